```python
import jax, jax.numpy as jnp
from jax import lax
import numpy as np

D_MODEL = 1024
BATCH = 8
SEQ = 8192
DEPTH = 1

GRID_W = 64
HEAD_DIM = 64
A_HEADS = D_MODEL // (2 * HEAD_DIM)
A_KV_HEADS = A_HEADS // 4
A_WINDOW = 128
A_BLOCK = 128
B_HEADS = D_MODEL // (2 * HEAD_DIM)
B_KH_MAX = 8
B_KW = 16
D_FF = 4 * D_MODEL
EPS = 1e-6
A_WIDTH = A_HEADS * HEAD_DIM
A_KV_WIDTH = A_KV_HEADS * HEAD_DIM
B_WIDTH = B_HEADS * HEAD_DIM
IN_WIDTH = A_WIDTH + 2 * A_KV_WIDTH + 3 * B_WIDTH + 2 * D_MODEL

kernel_name = "hybrid_window_gqa_natten_gated_encoder"


def rms_norm(x, g):
    xf = x.astype(jnp.float32)
    var = jnp.mean(xf * xf, axis=-1, keepdims=True)
    return (xf * lax.rsqrt(var + EPS)).astype(x.dtype) * g


def alibi_slopes(n):
    return 2.0 ** (-8.0 * jnp.arange(1, n + 1, dtype=jnp.float32) / n)


def windowed_gqa(q, k, v, sink):
    b, s, _, d = q.shape
    nb = s // A_BLOCK
    grp = A_HEADS // A_KV_HEADS
    qb = q.reshape(b, nb, A_BLOCK, A_KV_HEADS, grp, d)
    pad = ((0, 0), (A_BLOCK, A_BLOCK), (0, 0), (0, 0))
    kp = jnp.pad(k, pad)
    vp = jnp.pad(v, pad)
    slopes = alibi_slopes(A_HEADS).reshape(A_KV_HEADS, grp)
    sink_f = sink.astype(jnp.float32).reshape(A_KV_HEADS, grp)
    scale = d ** -0.5

    def block(i):
        qi = qb[:, i]
        ki = lax.dynamic_slice_in_dim(kp, i * A_BLOCK, 3 * A_BLOCK, axis=1)
        vi = lax.dynamic_slice_in_dim(vp, i * A_BLOCK, 3 * A_BLOCK, axis=1)
        q_pos = i * A_BLOCK + jnp.arange(A_BLOCK)
        k_pos = (i - 1) * A_BLOCK + jnp.arange(3 * A_BLOCK)
        dist = jnp.abs(q_pos[:, None] - k_pos[None, :])
        valid = (dist <= A_WINDOW) & (k_pos[None, :] >= 0) & (k_pos[None, :] < s)
        sc = jnp.einsum('bqkgd,bskd->bkgqs', qi, ki).astype(jnp.float32) * scale
        sc = sc - slopes[:, :, None, None] * dist.astype(jnp.float32)
        sc = jnp.where(valid, sc, -jnp.inf)
        sink_col = jnp.broadcast_to(sink_f[None, :, :, None, None], sc.shape[:-1] + (1,))
        p = jax.nn.softmax(jnp.concatenate([sc, sink_col], axis=-1), axis=-1)[..., :-1]
        return jnp.einsum('bkgqs,bskd->bqkgd', p.astype(v.dtype), vi)

    out = lax.map(block, jnp.arange(nb))
    return jnp.moveaxis(out, 0, 1).reshape(b, s, A_WIDTH)


def neighborhood_attn(q, k, v, rpb):
    b, s, h, d = q.shape
    rows = s // GRID_W
    kh = min(B_KH_MAX, rows)
    qg = q.reshape(b, rows, GRID_W, h, d)
    kg = k.reshape(b, rows, GRID_W, h, d)
    vg = v.reshape(b, rows, GRID_W, h, d)
    cols = jnp.arange(GRID_W)
    col_start = jnp.clip(cols - B_KW // 2, 0, GRID_W - B_KW)
    col_idx = col_start[:, None] + jnp.arange(B_KW)[None, :]
    dc = col_idx - cols[:, None] + (B_KW - 1)
    rpb_f = rpb.astype(jnp.float32)
    scale = d ** -0.5

    def row(r):
        r0 = jnp.clip(r - kh // 2, 0, rows - kh)
        kr = lax.dynamic_slice_in_dim(kg, r0, kh, axis=1)
        vr = lax.dynamic_slice_in_dim(vg, r0, kh, axis=1)
        kn = kr[:, :, col_idx]
        vn = vr[:, :, col_idx]
        qr = lax.dynamic_index_in_dim(qg, r, axis=1, keepdims=False)
        sc = jnp.einsum('bchd,brckhd->bhcrk', qr, kn).astype(jnp.float32) * scale
        dr = r0 + jnp.arange(kh) - r + (B_KH_MAX - 1)
        bias = rpb_f[:, dr[:, None, None], dc[None, :, :]]
        sc = sc + jnp.transpose(bias, (0, 2, 1, 3))[None]
        p = jax.nn.softmax(sc.reshape(b, h, GRID_W, kh * B_KW), axis=-1)
        p = p.reshape(b, h, GRID_W, kh, B_KW).astype(v.dtype)
        return jnp.einsum('bhcrk,brckhd->bchd', p, vn)

    out = lax.map(row, jnp.arange(rows))
    return jnp.moveaxis(out, 0, 1).reshape(b, s, B_WIDTH)


def setup_inputs(seed: int = 0) -> dict:
    key = jax.random.key(seed)
    ks = jax.random.split(key, 14)
    f32 = jnp.float32

    def nrm(k, shape, scale):
        return jax.random.normal(k, shape, f32) * scale

    return {
        "x": nrm(ks[0], (BATCH, SEQ, D_MODEL), 1.0),
        "norm_mix": 1.0 + nrm(ks[1], (DEPTH, D_MODEL), 0.02),
        "w_in": nrm(ks[2], (DEPTH, D_MODEL, IN_WIDTH), D_MODEL ** -0.5),
        "b_gate": nrm(ks[3], (DEPTH, 2 * D_MODEL), 0.01),
        "sink": nrm(ks[4], (DEPTH, A_HEADS), 0.5),
        "rpb": nrm(ks[5], (DEPTH, B_HEADS, 2 * B_KH_MAX - 1, 2 * B_KW - 1), 0.1),
        "w_proj_a": nrm(ks[6], (DEPTH, A_WIDTH, D_MODEL), A_WIDTH ** -0.5),
        "w_proj_b": nrm(ks[7], (DEPTH, B_WIDTH, D_MODEL), B_WIDTH ** -0.5),
        "w_out": nrm(ks[8], (DEPTH, D_MODEL, D_MODEL), D_MODEL ** -0.5),
        "norm_mlp": 1.0 + nrm(ks[9], (DEPTH, D_MODEL), 0.02),
        "w_up": nrm(ks[10], (DEPTH, D_MODEL, D_FF), D_MODEL ** -0.5),
        "w_down": nrm(ks[11], (DEPTH, D_FF, D_MODEL), D_FF ** -0.5),
        "norm_final": 1.0 + nrm(ks[12], (D_MODEL,), 0.02),
    }


def reference(x, norm_mix, w_in, b_gate, sink, rpb, w_proj_a, w_proj_b, w_out,
              norm_mlp, w_up, w_down, norm_final):
    b, s, _ = x.shape
    sizes = [A_WIDTH, A_KV_WIDTH, A_KV_WIDTH, B_WIDTH, B_WIDTH, B_WIDTH, D_MODEL, D_MODEL]
    offsets = [int(o) for o in np.cumsum(sizes)[:-1]]
    for l in range(DEPTH):
        h = rms_norm(x, norm_mix[l])
        z = h @ w_in[l]
        q_a, k_a, v_a, q_b, k_b, v_b, g_a, g_b = jnp.split(z, offsets, axis=-1)
        y_a = windowed_gqa(q_a.reshape(b, s, A_HEADS, HEAD_DIM),
                           k_a.reshape(b, s, A_KV_HEADS, HEAD_DIM),
                           v_a.reshape(b, s, A_KV_HEADS, HEAD_DIM), sink[l])
        y_b = neighborhood_attn(q_b.reshape(b, s, B_HEADS, HEAD_DIM),
                                k_b.reshape(b, s, B_HEADS, HEAD_DIM),
                                v_b.reshape(b, s, B_HEADS, HEAD_DIM), rpb[l])
        gates = jax.nn.sigmoid((jnp.concatenate([g_a, g_b], axis=-1) + b_gate[l])
                               .astype(jnp.float32)).astype(x.dtype)
        merged = (gates[..., :D_MODEL] * (y_a @ w_proj_a[l])
                  + gates[..., D_MODEL:] * (y_b @ w_proj_b[l]))
        x = x + merged @ w_out[l]
        h2 = rms_norm(x, norm_mlp[l])
        x = x + jnp.square(jax.nn.relu(h2 @ w_up[l])) @ w_down[l]
    return rms_norm(x, norm_final)
```

```python
import functools

import numpy as np
import jax
import jax.numpy as jnp
from jax import lax
from jax.experimental import pallas as pl
from jax.experimental.pallas import tpu as pltpu

HEAD_DIM = 64
LANES = 128
A_KV_HEADS = 2
A_WINDOW = 128
GRID_W = 64
B_KH = 8
B_KW = 16
EPS = 1e-6
NEG = -1e30

TOKEN_TILE = 512
WIN_TILE = 512
NBR_ROWS = 8
FF_CHUNK = 512
VMEM_LIMIT = 56 * 1024 * 1024

_BF = jnp.bfloat16
_F32 = jnp.float32
_NT = (((1,), (1,)), ((), ()))


def _rms(x, g):
    var = jnp.mean(x * x, axis=-1, keepdims=True)
    return (x * lax.rsqrt(var + EPS)) * g


def _resident(shape):
    zeros = (0,) * len(shape)
    return pl.BlockSpec(shape, lambda t: zeros, pipeline_mode=pl.Buffered(1))


def _in_proj_kernel(x_ref, g_ref, w_ref, bg_ref, qa_ref, kva_ref, qb_ref, kvb_ref, gate_ref):
    hb = _rms(x_ref[...], g_ref[...]).astype(_BF)
    off = 0
    for ref in (qa_ref, kva_ref, qb_ref, kvb_ref):
        width = ref.shape[1]
        ref[...] = jnp.dot(hb, w_ref[:, off:off + width], preferred_element_type=_F32).astype(_BF)
        off += width
    gate_w = gate_ref.shape[1]
    step = 512
    for c in range(0, gate_w, step):
        g = jnp.dot(hb, w_ref[:, off + c:off + c + step], preferred_element_type=_F32)
        gate_ref[:, c:c + step] = jax.nn.sigmoid(g + bg_ref[:, c:c + step]).astype(_BF)


def _in_proj(xf, g, w, bg, widths):
    n, d = xf.shape
    tm = min(TOKEN_TILE, n)
    out_shape = [jax.ShapeDtypeStruct((n, wd), _BF) for wd in widths]
    return pl.pallas_call(
        _in_proj_kernel,
        grid=(n // tm,),
        in_specs=[pl.BlockSpec((tm, d), lambda t: (t, 0)),
                  _resident((1, d)), _resident(w.shape), _resident(bg.shape)],
        out_specs=[pl.BlockSpec((tm, wd), lambda t: (t, 0)) for wd in widths],
        out_shape=out_shape,
        compiler_params=pltpu.CompilerParams(dimension_semantics=("arbitrary",),
                                             vmem_limit_bytes=VMEM_LIMIT),
        name="in_proj",
    )(xf, g, w, bg)


def _split_heads(qp):
    lo = lax.broadcasted_iota(jnp.int32, qp.shape, 1) < HEAD_DIM
    zero = jnp.zeros_like(qp)
    return jnp.concatenate([jnp.where(lo, qp, zero), jnp.where(lo, zero, qp)], axis=0)


def _block_diag(v):
    return _split_heads(v)


def _pair_out(p_lo, l_lo, p_hi, l_hi, vbd):
    o = jnp.dot(jnp.concatenate([p_lo, p_hi], axis=1), vbd, preferred_element_type=_F32)
    lo = lax.broadcasted_iota(jnp.int32, o.shape, 1) < HEAD_DIM
    return (o / jnp.where(lo, l_lo, l_hi)).astype(_BF)


def _win_attn_kernel(sink_ref, q_ref, kp_ref, kc_ref, kn_ref, bias_ref, o_ref, win_ref, *, tiles_per_batch):
    tq = q_ref.shape[0]
    blk = A_WINDOW
    nsub = tq // blk
    i = pl.program_id(0) % tiles_per_batch
    win_ref[0:blk] = kp_ref[...]
    win_ref[blk:blk + tq] = kc_ref[...]
    win_ref[blk + tq:] = kn_ref[...]
    col = lax.broadcasted_iota(jnp.int32, (1, 3 * blk), 1)
    edge_first = jnp.where(col < blk, jnp.where(i == 0, NEG, 0.0), 0.0)
    edge_last = jnp.where(col >= 2 * blk, jnp.where(i == tiles_per_batch - 1, NEG, 0.0), 0.0)
    n_pairs = q_ref.shape[1] // LANES
    for j in range(nsub):
        kv = win_ref[j * blk:(j + 3) * blk, :]
        k = kv[:, :LANES]
        vbd = _block_diag(kv[:, LANES:])
        qstack = jnp.concatenate(
            [_split_heads(q_ref[j * blk:(j + 1) * blk, p * LANES:(p + 1) * LANES]) for p in range(n_pairs)],
            axis=0)
        s_all = lax.dot_general(qstack, k, _NT, preferred_element_type=_F32)
        for p in range(n_pairs):
            stats = []
            for half, h in ((0, p), (1, p + n_pairs)):
                row0 = (2 * p + half) * blk
                s = s_all[row0:row0 + blk] + bias_ref[h]
                if j == 0:
                    s = s + edge_first
                if j == nsub - 1:
                    s = s + edge_last
                sink = sink_ref[h]
                m = jnp.maximum(jnp.max(s, axis=-1, keepdims=True), sink)
                e = jnp.exp(s - m)
                l = jnp.sum(e, axis=-1, keepdims=True) + jnp.exp(sink - m)
                stats += [e.astype(_BF), l]
            o_ref[j * blk:(j + 1) * blk, p * LANES:(p + 1) * LANES] = _pair_out(*stats, vbd)


def _win_attn(sink, qa, kva, bias, seq):
    n = qa.shape[0]
    tq = min(WIN_TILE, seq)
    blk = A_WINDOW
    sub = tq // blk
    n_blk = n // blk
    kernel = functools.partial(_win_attn_kernel, tiles_per_batch=seq // tq)
    return pl.pallas_call(
        kernel,
        grid=(n // tq,),
        in_specs=[pl.BlockSpec(memory_space=pltpu.SMEM),
                  pl.BlockSpec((tq, qa.shape[1]), lambda t: (t, 0)),
                  pl.BlockSpec((blk, kva.shape[1]), lambda t: (jnp.maximum(t * sub - 1, 0), 0)),
                  pl.BlockSpec((tq, kva.shape[1]), lambda t: (t, 0)),
                  pl.BlockSpec((blk, kva.shape[1]), lambda t: (jnp.minimum((t + 1) * sub, n_blk - 1), 0)),
                  _resident(bias.shape)],
        out_specs=pl.BlockSpec((tq, qa.shape[1]), lambda t: (t, 0)),
        out_shape=jax.ShapeDtypeStruct(qa.shape, _BF),
        scratch_shapes=[pltpu.VMEM((tq + 2 * blk, kva.shape[1]), _BF)],
        compiler_params=pltpu.CompilerParams(dimension_semantics=("arbitrary",),
                                             vmem_limit_bytes=VMEM_LIMIT),
        name="win_attn",
    )(sink, qa, kva, kva, kva, bias)


def _win_bias(n_heads):
    slopes = 2.0 ** (-8.0 * jnp.arange(1, n_heads + 1, dtype=_F32) / n_heads)
    q_pos = jnp.arange(A_WINDOW)[:, None]
    k_pos = jnp.arange(3 * A_WINDOW)[None, :] - A_WINDOW
    dist = jnp.abs(q_pos - k_pos)
    bias = -slopes[:, None, None] * dist.astype(_F32)[None]
    return jnp.where((dist <= A_WINDOW)[None], bias, NEG)


def _nbr_attn_kernel(q_ref, win_ref, bias_ref, o_ref, *, rows, tiles_per_batch):
    rt = q_ref.shape[0] // GRID_W
    win_rows = win_ref.shape[0] // GRID_W
    width = q_ref.shape[1]
    n_pairs = width // LANES
    first_row = (pl.program_id(0) % tiles_per_batch) * rt
    win_row0 = jnp.clip(first_row - B_KH // 2, 0, rows - win_rows)

    def row_body(lr, carry):
        r = first_row + lr
        r0 = jnp.clip(r - B_KH // 2, 0, rows - B_KH)
        start = pl.multiple_of((r0 - win_row0) * GRID_W, GRID_W)
        case = r0 - r + (B_KH - 1)
        q_off = pl.multiple_of(lr * GRID_W, GRID_W)
        for p in range(n_pairs):
            cols = slice(p * LANES, (p + 1) * LANES)
            qs = _split_heads(q_ref[pl.ds(q_off, GRID_W), cols])
            k = win_ref[pl.ds(start, B_KH * GRID_W), cols]
            v = win_ref[pl.ds(start, B_KH * GRID_W), width + p * LANES:width + (p + 1) * LANES]
            s = lax.dot_general(qs, k, _NT, preferred_element_type=_F32)
            stats = []
            for half in range(2):
                sh = s[half * GRID_W:(half + 1) * GRID_W] + bias_ref[case, 2 * p + half]
                m = jnp.max(sh, axis=-1, keepdims=True)
                e = jnp.exp(sh - m)
                stats += [e.astype(_BF), jnp.sum(e, axis=-1, keepdims=True)]
            o_ref[pl.ds(q_off, GRID_W), cols] = _pair_out(*stats, _block_diag(v))
        return carry

    lax.fori_loop(0, rt, row_body, 0)


def _nbr_attn(qb, kvb, bias, seq):
    n, width = qb.shape
    rows = seq // GRID_W
    rt = min(NBR_ROWS, rows - B_KH) if rows > B_KH else rows
    win_rows = min(rt + B_KH, rows)
    tile = rt * GRID_W
    tiles_per_batch = seq // tile
    kernel = functools.partial(_nbr_attn_kernel, rows=rows, tiles_per_batch=tiles_per_batch)

    def win_index(t):
        first_row = (t % tiles_per_batch) * rt
        win_row0 = jnp.clip(first_row - B_KH // 2, 0, rows - win_rows)
        return (pl.multiple_of((t // tiles_per_batch) * seq + win_row0 * GRID_W, GRID_W), 0)

    return pl.pallas_call(
        kernel,
        grid=(n // tile,),
        in_specs=[pl.BlockSpec((tile, width), lambda t: (t, 0)),
                  pl.BlockSpec((pl.Element(win_rows * GRID_W), pl.Element(kvb.shape[1])), win_index),
                  _resident(bias.shape)],
        out_specs=pl.BlockSpec((tile, width), lambda t: (t, 0)),
        out_shape=jax.ShapeDtypeStruct(qb.shape, _BF),
        compiler_params=pltpu.CompilerParams(dimension_semantics=("arbitrary",),
                                             vmem_limit_bytes=VMEM_LIMIT),
        name="nbr_attn",
    )(qb, kvb, bias)


def _nbr_bias(rpb):
    c = jnp.arange(GRID_W)[:, None]
    kc = jnp.arange(GRID_W)[None, :]
    col_start = jnp.clip(c - B_KW // 2, 0, GRID_W - B_KW)
    valid = (kc >= col_start) & (kc < col_start + B_KW)
    dc = jnp.clip(kc - c + (B_KW - 1), 0, 2 * B_KW - 2)
    g = jnp.where(valid[None, None], rpb.astype(_F32)[:, :, dc], NEG)
    tab = jnp.stack([g[:, e:e + B_KH] for e in range(B_KH)])
    n_heads = rpb.shape[0]
    return tab.transpose(0, 1, 3, 2, 4).reshape(B_KH, n_heads, GRID_W, B_KH * GRID_W)


def _out_mlp_kernel(x_ref, ya_ref, yb_ref, gate_ref, wpa_ref, wpb_ref, wo_ref, g2_ref, wup_ref, wdn_ref,
                    gf_ref, o_ref, *, final_norm):
    d = x_ref.shape[1]
    pa = jnp.dot(ya_ref[...], wpa_ref[...], preferred_element_type=_F32)
    pb = jnp.dot(yb_ref[...], wpb_ref[...], preferred_element_type=_F32)
    merged = gate_ref[:, :d].astype(_F32) * pa + gate_ref[:, d:].astype(_F32) * pb
    x1 = x_ref[...] + jnp.dot(merged.astype(_BF), wo_ref[...], preferred_element_type=_F32)
    h2 = _rms(x1, g2_ref[...]).astype(_BF)
    acc = x1
    d_ff = wup_ref.shape[1]
    for c in range(0, d_ff, FF_CHUNK):
        u = jnp.dot(h2, wup_ref[:, c:c + FF_CHUNK], preferred_element_type=_F32)
        a = jnp.square(jnp.maximum(u, 0.0)).astype(_BF)
        acc = acc + jnp.dot(a, wdn_ref[c:c + FF_CHUNK, :], preferred_element_type=_F32)
    o_ref[...] = _rms(acc, gf_ref[...]) if final_norm else acc


def _out_mlp(xf, ya, yb, gates, wpa, wpb, wo, g2, wup, wdn, gf, final_norm):
    n, d = xf.shape
    tm = min(TOKEN_TILE, n)
    row = lambda width: pl.BlockSpec((tm, width), lambda t: (t, 0))
    return pl.pallas_call(
        functools.partial(_out_mlp_kernel, final_norm=final_norm),
        grid=(n // tm,),
        in_specs=[row(d), row(ya.shape[1]), row(yb.shape[1]), row(gates.shape[1]),
                  _resident(wpa.shape), _resident(wpb.shape), _resident(wo.shape), _resident((1, d)),
                  _resident(wup.shape), _resident(wdn.shape), _resident((1, d))],
        out_specs=row(d),
        out_shape=jax.ShapeDtypeStruct((n, d), _F32),
        compiler_params=pltpu.CompilerParams(dimension_semantics=("arbitrary",),
                                             vmem_limit_bytes=VMEM_LIMIT),
        name="out_mlp",
    )(xf, ya, yb, gates, wpa, wpb, wo, g2, wup, wdn, gf)


def kernel(x, norm_mix, w_in, b_gate, sink, rpb, w_proj_a, w_proj_b, w_out, norm_mlp, w_up, w_down, norm_final):
    b, s, d = x.shape
    depth = w_in.shape[0]
    a_heads = sink.shape[1]
    b_heads = rpb.shape[1]
    a_w, kv_w, b_w = a_heads * HEAD_DIM, A_KV_HEADS * HEAD_DIM, b_heads * HEAD_DIM
    pair_order = np.array([h for p in range(a_heads // 2) for h in (p, p + a_heads // 2)])
    scale = HEAD_DIM ** -0.5
    win_bias = _win_bias(a_heads)
    xf = x.reshape(b * s, d)
    for l in range(depth):
        w = w_in[l]
        wqa = w[:, :a_w].reshape(d, a_heads, HEAD_DIM)[:, pair_order].reshape(d, a_w) * scale
        qb0 = a_w + 2 * kv_w
        w = jnp.concatenate([wqa, w[:, a_w:qb0], w[:, qb0:qb0 + b_w] * scale, w[:, qb0 + b_w:]], axis=1)
        widths = (a_w, 2 * kv_w, b_w, 2 * b_w, 2 * d)
        qa, kva, qb, kvb, gates = _in_proj(xf, norm_mix[l][None], w.astype(_BF), b_gate[l][None], widths)
        ya = _win_attn(sink[l].astype(_F32), qa, kva, win_bias, s)
        yb = _nbr_attn(qb, kvb, _nbr_bias(rpb[l]), s)
        wpa = w_proj_a[l].reshape(a_heads, HEAD_DIM, d)[pair_order].reshape(a_w, d)
        xf = _out_mlp(xf, ya, yb, gates, wpa.astype(_BF), w_proj_b[l].astype(_BF), w_out[l].astype(_BF),
                      norm_mlp[l][None], w_up[l].astype(_BF), w_down[l].astype(_BF), norm_final[None],
                      final_norm=(l == depth - 1))
    return xf.reshape(b, s, d)
```

```python
import functools

import numpy as np
import jax
import jax.numpy as jnp
from jax import lax
from jax.experimental import pallas as pl
from jax.experimental.pallas import tpu as pltpu

HEAD_DIM = 64
LANES = 128
CHUNK = 128
A_KV_HEADS = 2
A_WINDOW = 128
GRID_W = 64
B_KH = 8
B_KW = 16
NBR_KEY_ROWS = 10
EPS = 1e-6
NEG = -1e30
ONES_ROWS = 16

TOKEN_TILE = 512
WIN_TILE = 512
NBR_ROWS = 8
FF_CHUNK = 512
VMEM_LIMIT = 56 * 1024 * 1024

_BF = jnp.bfloat16
_F32 = jnp.float32
_NT = (((1,), (1,)), ((), ()))


def _rms(x, g):
    var = jnp.mean(x * x, axis=-1, keepdims=True)
    return (x * lax.rsqrt(var + EPS)) * g


def _resident(shape):
    zeros = (0,) * len(shape)
    return pl.BlockSpec(shape, lambda t: zeros, pipeline_mode=pl.Buffered(1))


def _params():
    return pltpu.CompilerParams(dimension_semantics=("arbitrary",), vmem_limit_bytes=VMEM_LIMIT)


def _in_proj_kernel(x_ref, g_ref, w_ref, wvt_ref, bg_ref, qa_ref, ka_ref, qb_ref, kb_ref, gate_ref,
                    vta_ref, vtb_ref):
    hb = _rms(x_ref[...], g_ref[...]).astype(_BF)
    off = 0
    for ref in (qa_ref, ka_ref, qb_ref, kb_ref):
        width = ref.shape[1]
        ref[...] = jnp.dot(hb, w_ref[:, off:off + width], preferred_element_type=_F32).astype(_BF)
        off += width
    gate_w = gate_ref.shape[1]
    step = 512
    for c in range(0, gate_w, step):
        g = jnp.dot(hb, w_ref[:, off + c:off + c + step], preferred_element_type=_F32)
        gate_ref[:, c:c + step] = jax.nn.sigmoid(g + bg_ref[:, c:c + step]).astype(_BF)
    vt = lax.dot_general(wvt_ref[...], hb, _NT, preferred_element_type=_F32).astype(_BF)
    va = vta_ref.shape[1]
    for c in range(vta_ref.shape[0]):
        vta_ref[c] = vt[:va, c * CHUNK:(c + 1) * CHUNK]
        vtb_ref[c] = vt[va:, c * CHUNK:(c + 1) * CHUNK]


def _in_proj(xf, g, w, wvt, bg, widths, v_widths):
    n, d = xf.shape
    tm = min(TOKEN_TILE, n)
    out_shape = [jax.ShapeDtypeStruct((n, wd), _BF) for wd in widths]
    out_shape += [jax.ShapeDtypeStruct((n // CHUNK, vw, CHUNK), _BF) for vw in v_widths]
    out_specs = [pl.BlockSpec((tm, wd), lambda t: (t, 0)) for wd in widths]
    out_specs += [pl.BlockSpec((tm // CHUNK, vw, CHUNK), lambda t: (t, 0, 0)) for vw in v_widths]
    return pl.pallas_call(
        _in_proj_kernel,
        grid=(n // tm,),
        in_specs=[pl.BlockSpec((tm, d), lambda t: (t, 0)),
                  _resident((1, d)), _resident(w.shape), _resident(wvt.shape), _resident(bg.shape)],
        out_specs=out_specs,
        out_shape=out_shape,
        compiler_params=_params(),
        name="in_proj",
    )(xf, g, w, wvt, bg)


def _with_ones(vt):
    return jnp.concatenate([vt, jnp.ones((ONES_ROWS, vt.shape[1]), vt.dtype)], axis=0)


def _attend(heads, s_scr, p_scr):
    n_chunks = s_scr.shape[1] // CHUNK
    maxes = []
    for i, (k_chunk, qm, bias_chunk, _, sink) in enumerate(heads):
        mx = None
        for c in range(n_chunks):
            s_c = lax.dot_general(k_chunk(c), qm, _NT, preferred_element_type=_F32) + bias_chunk(c)
            s_scr[i, c * CHUNK:(c + 1) * CHUNK] = s_c
            mx = s_c if mx is None else jnp.maximum(mx, s_c)
        m = jnp.max(mx, axis=0, keepdims=True)
        maxes.append(m if sink is None else jnp.maximum(m, sink))
    for i, m in enumerate(maxes):
        for c in range(n_chunks):
            rows = slice(c * CHUNK, (c + 1) * CHUNK)
            p_scr[i, rows] = jnp.exp(s_scr[i, rows] - m).astype(_BF)
    outs = []
    for i, (_, _, _, vt_aug, sink) in enumerate(heads):
        oa = jnp.dot(vt_aug, p_scr[i], preferred_element_type=_F32)
        l = oa[HEAD_DIM:HEAD_DIM + 1]
        if sink is not None:
            l = l + jnp.exp(sink - maxes[i])
        outs.append(oa[:HEAD_DIM] * (1.0 / l))
    return outs


def _one_head(q_tile, half):
    lane = lax.broadcasted_iota(jnp.int32, q_tile.shape, 1)
    keep = (lane < HEAD_DIM) if half == 0 else (lane >= HEAD_DIM)
    return jnp.where(keep, q_tile, jnp.zeros_like(q_tile))


def _win_attn_kernel(sink_ref, q_ref, kp_ref, kc_ref, kn_ref, vp_ref, vc_ref, vn_ref, bias_ref, o_ref,
                     s_scr, p_scr, *, tiles_per_batch):
    blk = A_WINDOW
    nsub = q_ref.shape[0] // blk
    n_pairs = q_ref.shape[1] // LANES
    i = pl.program_id(0) % tiles_per_batch
    first_neg = jnp.where(i == 0, NEG, 0.0)
    last_neg = jnp.where(i == tiles_per_batch - 1, NEG, 0.0)
    for j in range(nsub):
        k_blocks = [kp_ref[...] if j == 0 else kc_ref[(j - 1) * blk:j * blk],
                    kc_ref[j * blk:(j + 1) * blk],
                    kn_ref[...] if j == nsub - 1 else kc_ref[(j + 1) * blk:(j + 2) * blk]]
        v_t = jnp.concatenate(
            [vp_ref[0] if j == 0 else vc_ref[j - 1], vc_ref[j],
             vn_ref[0] if j == nsub - 1 else vc_ref[j + 1]], axis=1)
        vt_aug = [_with_ones(v_t[kv * HEAD_DIM:(kv + 1) * HEAD_DIM]) for kv in range(A_KV_HEADS)]
        heads = []
        for p in range(n_pairs):
            q_tile = q_ref[j * blk:(j + 1) * blk, p * LANES:(p + 1) * LANES]
            for half, h in ((0, p), (1, p + n_pairs)):

                def bias_chunk(c, h=h, j=j):
                    b = bias_ref[h, c * blk:(c + 1) * blk]
                    if j == 0 and c == 0:
                        b = b + first_neg
                    if j == nsub - 1 and c == 2:
                        b = b + last_neg
                    return b

                heads.append((lambda c, kb=k_blocks: kb[c], _one_head(q_tile, half), bias_chunk,
                              vt_aug[half], sink_ref[h]))
        outs = _attend(heads, s_scr, p_scr)
        for p in range(n_pairs):
            o_ref[j * blk:(j + 1) * blk, p * LANES:(p + 1) * LANES] = (
                jnp.concatenate(outs[2 * p:2 * p + 2], axis=0).T.astype(_BF))


def _win_attn(sink, qa, ka, vta, bias, seq):
    n = qa.shape[0]
    tq = min(WIN_TILE, seq)
    blk = A_WINDOW
    sub = tq // blk
    n_blk = n // blk
    prev_blk = lambda t: jnp.maximum(t * sub - 1, 0)
    next_blk = lambda t: jnp.minimum((t + 1) * sub, n_blk - 1)
    kernel = functools.partial(_win_attn_kernel, tiles_per_batch=seq // tq)
    kw, vw = ka.shape[1], vta.shape[1]
    return pl.pallas_call(
        kernel,
        grid=(n // tq,),
        in_specs=[pl.BlockSpec(memory_space=pltpu.SMEM),
                  pl.BlockSpec((tq, qa.shape[1]), lambda t: (t, 0)),
                  pl.BlockSpec((blk, kw), lambda t: (prev_blk(t), 0)),
                  pl.BlockSpec((tq, kw), lambda t: (t, 0)),
                  pl.BlockSpec((blk, kw), lambda t: (next_blk(t), 0)),
                  pl.BlockSpec((1, vw, CHUNK), lambda t: (prev_blk(t), 0, 0)),
                  pl.BlockSpec((sub, vw, CHUNK), lambda t: (t, 0, 0)),
                  pl.BlockSpec((1, vw, CHUNK), lambda t: (next_blk(t), 0, 0)),
                  _resident(bias.shape)],
        out_specs=pl.BlockSpec((tq, qa.shape[1]), lambda t: (t, 0)),
        out_shape=jax.ShapeDtypeStruct(qa.shape, _BF),
        scratch_shapes=[pltpu.VMEM((bias.shape[0], 3 * blk, LANES), _F32),
                        pltpu.VMEM((bias.shape[0], 3 * blk, LANES), _BF)],
        compiler_params=_params(),
        name="win_attn",
    )(sink, qa, ka, ka, ka, vta, vta, vta, bias)


def _win_bias(n_heads):
    slopes = 2.0 ** (-8.0 * jnp.arange(1, n_heads + 1, dtype=_F32) / n_heads)
    k_pos = jnp.arange(3 * A_WINDOW)[:, None] - A_WINDOW
    q_pos = jnp.arange(A_WINDOW)[None, :]
    dist = jnp.abs(q_pos - k_pos)
    bias = -slopes[:, None, None] * dist.astype(_F32)[None]
    return jnp.where((dist <= A_WINDOW)[None], bias, NEG)


def _nbr_attn_kernel(q_ref, kwin_ref, vwin_ref, bias_ref, o_ref, s_scr, p_scr, *, rows, tiles_per_batch):
    rt = q_ref.shape[0] // GRID_W
    win_rows = kwin_ref.shape[0] // GRID_W
    n_pairs = q_ref.shape[1] // LANES
    n_keys = NBR_KEY_ROWS * GRID_W
    n_chunks = n_keys // CHUNK
    first_row = (pl.program_id(0) % tiles_per_batch) * rt
    win_row0 = jnp.clip(first_row - B_KH // 2, 0, rows - win_rows)

    def pair_body(jp, carry):
        r = first_row + 2 * jp
        u0 = jnp.clip(r - B_KH // 2, 0, rows - NBR_KEY_ROWS)
        k_off = pl.multiple_of((u0 - win_row0) * GRID_W, CHUNK)
        c_off = (u0 - win_row0) // 2
        combo = jnp.where(r == 0, 1, jnp.where(r == 2, 2, jnp.where(r == rows - 4, 3,
                                                                    jnp.where(r == rows - 2, 4, 0))))
        q_off = pl.multiple_of(jp * 2 * GRID_W, 2 * GRID_W)
        v_chunks = [vwin_ref[c_off + c] for c in range(n_chunks)]
        heads = []
        for p in range(n_pairs):
            cols = slice(p * LANES, (p + 1) * LANES)
            q_tile = q_ref[pl.ds(q_off, 2 * GRID_W), cols]
            k_chunks = [kwin_ref[pl.ds(pl.multiple_of(k_off + c * CHUNK, CHUNK), CHUNK), cols]
                        for c in range(n_chunks)]
            for half in range(2):
                h = 2 * p + half
                v_t = jnp.concatenate([vc[h * HEAD_DIM:(h + 1) * HEAD_DIM] for vc in v_chunks], axis=1)
                heads.append((lambda c, kc=k_chunks: kc[c], _one_head(q_tile, half),
                              lambda c, h=h: bias_ref[combo, h, c * CHUNK:(c + 1) * CHUNK],
                              _with_ones(v_t), None))
        outs = _attend(heads, s_scr, p_scr)
        for p in range(n_pairs):
            o_ref[pl.ds(q_off, 2 * GRID_W), p * LANES:(p + 1) * LANES] = (
                jnp.concatenate(outs[2 * p:2 * p + 2], axis=0).T.astype(_BF))
        return carry

    lax.fori_loop(0, rt // 2, pair_body, 0)


def _nbr_attn(qb, kb, vtb, bias, seq):
    n, width = qb.shape
    rows = seq // GRID_W
    rt = NBR_ROWS
    win_rows = rt + B_KH
    assert rows >= win_rows and rows % rt == 0 and win_rows % 2 == 0
    tile = rt * GRID_W
    tiles_per_batch = seq // tile
    kernel = functools.partial(_nbr_attn_kernel, rows=rows, tiles_per_batch=tiles_per_batch)

    def win_token(t):
        first_row = (t % tiles_per_batch) * rt
        win_row0 = jnp.clip(first_row - B_KH // 2, 0, rows - win_rows)
        return (t // tiles_per_batch) * seq + win_row0 * GRID_W

    return pl.pallas_call(
        kernel,
        grid=(n // tile,),
        in_specs=[pl.BlockSpec((tile, width), lambda t: (t, 0)),
                  pl.BlockSpec((pl.Element(win_rows * GRID_W), pl.Element(width)),
                               lambda t: (pl.multiple_of(win_token(t), CHUNK), 0)),
                  pl.BlockSpec((pl.Element(win_rows * GRID_W // CHUNK), pl.Element(width), pl.Element(CHUNK)),
                               lambda t: (win_token(t) // CHUNK, 0, 0)),
                  _resident(bias.shape)],
        out_specs=pl.BlockSpec((tile, width), lambda t: (t, 0)),
        out_shape=jax.ShapeDtypeStruct(qb.shape, _BF),
        scratch_shapes=[pltpu.VMEM((bias.shape[1], NBR_KEY_ROWS * GRID_W, LANES), _F32),
                        pltpu.VMEM((bias.shape[1], NBR_KEY_ROWS * GRID_W, LANES), _BF)],
        compiler_params=_params(),
        name="nbr_attn",
    )(qb, kb, vtb, bias)


def _nbr_bias(rpb):
    variants = [(-4, 0, -5, 1), (0, 0, -1, 0), (-2, 0, -3, 0), (-6, 2, -7, 2), (-8, 2, -9, 2)]
    c = jnp.arange(GRID_W)[:, None]
    kc = jnp.arange(GRID_W)[None, :]
    col_start = jnp.clip(c - B_KW // 2, 0, GRID_W - B_KW)
    col_ok = (kc >= col_start) & (kc < col_start + B_KW)
    dc = jnp.clip(kc - c + (B_KW - 1), 0, 2 * B_KW - 2)
    kk = jnp.arange(NBR_KEY_ROWS)
    n_heads = rpb.shape[0]
    rpb = rpb.astype(_F32)

    def one_row(a, b):
        dr = jnp.clip(a + kk + (B_KH - 1), 0, 2 * B_KH - 2)
        row_ok = (kk >= b) & (kk < b + B_KH)
        t = rpb[:, dr][:, :, dc]
        t = jnp.where(row_ok[None, :, None, None] & col_ok[None, None], t, NEG)
        return t.transpose(0, 1, 3, 2).reshape(n_heads, NBR_KEY_ROWS * GRID_W, GRID_W)

    return jnp.stack([jnp.concatenate([one_row(a1, b1), one_row(a2, b2)], axis=-1)
                      for a1, b1, a2, b2 in variants])


def _out_mlp_kernel(x_ref, ya_ref, yb_ref, gate_ref, wpa_ref, wpb_ref, wo_ref, g2_ref, wup_ref, wdn_ref,
                    gf_ref, o_ref, *, final_norm):
    d = x_ref.shape[1]
    pa = jnp.dot(ya_ref[...], wpa_ref[...], preferred_element_type=_F32)
    pb = jnp.dot(yb_ref[...], wpb_ref[...], preferred_element_type=_F32)
    merged = gate_ref[:, :d].astype(_F32) * pa + gate_ref[:, d:].astype(_F32) * pb
    x1 = x_ref[...] + jnp.dot(merged.astype(_BF), wo_ref[...], preferred_element_type=_F32)
    h2 = _rms(x1, g2_ref[...]).astype(_BF)
    acc = x1
    d_ff = wup_ref.shape[1]
    for c in range(0, d_ff, FF_CHUNK):
        u = jnp.dot(h2, wup_ref[:, c:c + FF_CHUNK], preferred_element_type=_F32)
        a = jnp.square(jnp.maximum(u, 0.0)).astype(_BF)
        acc = acc + jnp.dot(a, wdn_ref[c:c + FF_CHUNK, :], preferred_element_type=_F32)
    o_ref[...] = _rms(acc, gf_ref[...]) if final_norm else acc


def _out_mlp(xf, ya, yb, gates, wpa, wpb, wo, g2, wup, wdn, gf, final_norm):
    n, d = xf.shape
    tm = min(TOKEN_TILE, n)
    row = lambda width: pl.BlockSpec((tm, width), lambda t: (t, 0))
    return pl.pallas_call(
        functools.partial(_out_mlp_kernel, final_norm=final_norm),
        grid=(n // tm,),
        in_specs=[row(d), row(ya.shape[1]), row(yb.shape[1]), row(gates.shape[1]),
                  _resident(wpa.shape), _resident(wpb.shape), _resident(wo.shape), _resident((1, d)),
                  _resident(wup.shape), _resident(wdn.shape), _resident((1, d))],
        out_specs=row(d),
        out_shape=jax.ShapeDtypeStruct((n, d), _F32),
        compiler_params=_params(),
        name="out_mlp",
    )(xf, ya, yb, gates, wpa, wpb, wo, g2, wup, wdn, gf)


def kernel(x, norm_mix, w_in, b_gate, sink, rpb, w_proj_a, w_proj_b, w_out, norm_mlp, w_up, w_down, norm_final):
    b, s, d = x.shape
    depth = w_in.shape[0]
    a_heads = sink.shape[1]
    b_heads = rpb.shape[1]
    a_w, kv_w, b_w = a_heads * HEAD_DIM, A_KV_HEADS * HEAD_DIM, b_heads * HEAD_DIM
    pair_order = np.array([h for p in range(a_heads // 2) for h in (p, p + a_heads // 2)])
    scale = HEAD_DIM ** -0.5
    win_bias = _win_bias(a_heads)
    xf = x.reshape(b * s, d)
    for l in range(depth):
        w = w_in[l]
        ka0, va0, qb0 = a_w, a_w + kv_w, a_w + 2 * kv_w
        kb0, vb0, g0 = qb0 + b_w, qb0 + 2 * b_w, qb0 + 3 * b_w
        wqa = w[:, :a_w].reshape(d, a_heads, HEAD_DIM)[:, pair_order].reshape(d, a_w) * scale
        w_main = jnp.concatenate([wqa, w[:, ka0:va0], w[:, qb0:kb0] * scale, w[:, kb0:vb0], w[:, g0:]], axis=1)
        w_vt = jnp.concatenate([w[:, va0:qb0], w[:, vb0:g0]], axis=1).T
        qa, ka, qb, kb, gates, vta, vtb = _in_proj(
            xf, norm_mix[l][None], w_main.astype(_BF), w_vt.astype(_BF), b_gate[l][None],
            (a_w, kv_w, b_w, b_w, 2 * d), (kv_w, b_w))
        ya = _win_attn(sink[l].astype(_F32), qa, ka, vta, win_bias, s)
        yb = _nbr_attn(qb, kb, vtb, _nbr_bias(rpb[l]), s)
        wpa = w_proj_a[l].reshape(a_heads, HEAD_DIM, d)[pair_order].reshape(a_w, d)
        xf = _out_mlp(xf, ya, yb, gates, wpa.astype(_BF), w_proj_b[l].astype(_BF), w_out[l].astype(_BF),
                      norm_mlp[l][None], w_up[l].astype(_BF), w_down[l].astype(_BF), norm_final[None],
                      final_norm=(l == depth - 1))
    return xf.reshape(b, s, d)
```

```python
import functools

import numpy as np
import jax
import jax.numpy as jnp
from jax import lax
from jax.experimental import pallas as pl
from jax.experimental.pallas import tpu as pltpu

HEAD_DIM = 64
LANES = 128
CHUNK = 128
A_KV_HEADS = 2
A_WINDOW = 128
GRID_W = 64
B_KH = 8
B_KW = 16
NBR_KEY_ROWS = 10
EPS = 1e-6
NEG = -1e30
LOG2E = 1.4426950408889634
ONES_ROWS = 16
PIPE_SLOTS = 4

TOKEN_TILE = 512
WIN_TILE = 512
NBR_ROWS = 8
FF_CHUNK = 512
VMEM_LIMIT = 56 * 1024 * 1024

_BF = jnp.bfloat16
_F32 = jnp.float32
_NT = (((1,), (1,)), ((), ()))


def _rms(x, g):
    var = jnp.mean(x * x, axis=-1, keepdims=True)
    return (x * lax.rsqrt(var + EPS)) * g


def _resident(shape):
    zeros = (0,) * len(shape)
    return pl.BlockSpec(shape, lambda t: zeros, pipeline_mode=pl.Buffered(1))


def _params():
    return pltpu.CompilerParams(dimension_semantics=("arbitrary",), vmem_limit_bytes=VMEM_LIMIT)


def _in_proj_kernel(x_ref, g_ref, w_ref, wvt_ref, bg_ref, qa_ref, ka_ref, qb_ref, kb_ref, gate_ref,
                    vta_ref, vtb_ref):
    hb = _rms(x_ref[...], g_ref[...]).astype(_BF)
    off = 0
    for ref in (qa_ref, ka_ref, qb_ref, kb_ref):
        width = ref.shape[1]
        ref[...] = jnp.dot(hb, w_ref[:, off:off + width], preferred_element_type=_F32).astype(_BF)
        off += width
    gate_w = gate_ref.shape[1]
    step = 512
    for c in range(0, gate_w, step):
        g = jnp.dot(hb, w_ref[:, off + c:off + c + step], preferred_element_type=_F32)
        gate_ref[:, c:c + step] = jax.nn.sigmoid(g + bg_ref[:, c:c + step]).astype(_BF)
    vt = lax.dot_general(wvt_ref[...], hb, _NT, preferred_element_type=_F32).astype(_BF)
    va = vta_ref.shape[1]
    for c in range(vta_ref.shape[0]):
        vta_ref[c] = vt[:va, c * CHUNK:(c + 1) * CHUNK]
        vtb_ref[c] = vt[va:, c * CHUNK:(c + 1) * CHUNK]


def _in_proj(xf, g, w, wvt, bg, widths, v_widths):
    n, d = xf.shape
    tm = min(TOKEN_TILE, n)
    out_shape = [jax.ShapeDtypeStruct((n, wd), _BF) for wd in widths]
    out_shape += [jax.ShapeDtypeStruct((n // CHUNK, vw, CHUNK), _BF) for vw in v_widths]
    out_specs = [pl.BlockSpec((tm, wd), lambda t: (t, 0)) for wd in widths]
    out_specs += [pl.BlockSpec((tm // CHUNK, vw, CHUNK), lambda t: (t, 0, 0)) for vw in v_widths]
    return pl.pallas_call(
        _in_proj_kernel,
        grid=(n // tm,),
        in_specs=[pl.BlockSpec((tm, d), lambda t: (t, 0)),
                  _resident((1, d)), _resident(w.shape), _resident(wvt.shape), _resident(bg.shape)],
        out_specs=out_specs,
        out_shape=out_shape,
        compiler_params=_params(),
        name="in_proj",
    )(xf, g, w, wvt, bg)


def _with_ones(vt):
    return jnp.concatenate([vt, jnp.ones((ONES_ROWS, vt.shape[1]), vt.dtype)], axis=0)


def _scores(k_chunk, qm, bias_chunk, sink, s_scr):
    mx = None
    n_chunks = s_scr.shape[0] // CHUNK
    k_all = jnp.concatenate([k_chunk(c) for c in range(n_chunks)], axis=0)
    s_all = lax.dot_general(k_all, qm, _NT, preferred_element_type=_F32)
    for c in range(n_chunks):
        s_c = s_all[c * CHUNK:(c + 1) * CHUNK] + bias_chunk(c)
        s_scr[c * CHUNK:(c + 1) * CHUNK] = s_c
        mx = s_c if mx is None else jnp.maximum(mx, s_c)
    m = jnp.max(mx, axis=0, keepdims=True)
    return m if sink is None else jnp.maximum(m, sink)


def _probs(m, s_scr, p_scr):
    for c in range(s_scr.shape[0] // CHUNK):
        rows = slice(c * CHUNK, (c + 1) * CHUNK)
        p_scr[rows] = jnp.exp2(s_scr[rows] - m).astype(_BF)


def _values(vt_aug, sink, m, p_scr):
    oa = jnp.dot(vt_aug, p_scr[...], preferred_element_type=_F32)
    l = oa[HEAD_DIM:HEAD_DIM + 1]
    if sink is not None:
        l = l + jnp.exp2(sink - m)
    return oa[:HEAD_DIM] * (1.0 / l)


def _pipelined(n_jobs, score_job, value_job, s_scr, p_scr):
    n_slots = s_scr.shape[0]
    maxes = {}
    for step in range(n_jobs + 2):
        if step < n_jobs:
            maxes[step] = score_job(step, s_scr.at[step % n_slots])
        if 0 <= step - 2 < n_jobs:
            value_job(step - 2, maxes.pop(step - 2), p_scr.at[(step - 2) % n_slots])
        if 0 <= step - 1 < n_jobs:
            _probs(maxes[step - 1], s_scr.at[(step - 1) % n_slots], p_scr.at[(step - 1) % n_slots])


def _one_head(q_tile, half):
    lane = lax.broadcasted_iota(jnp.int32, q_tile.shape, 1)
    keep = (lane < HEAD_DIM) if half == 0 else (lane >= HEAD_DIM)
    return jnp.where(keep, q_tile, jnp.zeros_like(q_tile))


def _win_attn_kernel(sink_ref, q_ref, kp_ref, kc_ref, kn_ref, vp_ref, vc_ref, vn_ref, bias_ref, o_ref,
                     s_scr, p_scr, *, tiles_per_batch):
    blk = A_WINDOW
    nsub = q_ref.shape[0] // blk
    n_pairs = q_ref.shape[1] // LANES
    i = pl.program_id(0) % tiles_per_batch
    first_neg = jnp.where(i == 0, NEG, 0.0)
    last_neg = jnp.where(i == tiles_per_batch - 1, NEG, 0.0)
    head_ids = [h for p in range(n_pairs) for h in (p, p + n_pairs)]

    n_heads = len(head_ids)
    pending = {}

    def score_job(n, s_slot):
        j, i = divmod(n, n_heads)
        h = head_ids[i]

        def k_chunk(c):
            b = j - 1 + c
            if b < 0:
                return kp_ref[...]
            return kn_ref[...] if b == nsub else kc_ref[b * blk:(b + 1) * blk]

        def bias_chunk(c):
            b = bias_ref[h, c * blk:(c + 1) * blk]
            if j == 0 and c == 0:
                b = b + first_neg
            if j == nsub - 1 and c == 2:
                b = b + last_neg
            return b

        q_tile = q_ref[j * blk:(j + 1) * blk, (i // 2) * LANES:(i // 2 + 1) * LANES]
        return _scores(k_chunk, _one_head(q_tile, i % 2), bias_chunk, sink_ref[h], s_slot)

    def value_job(n, m, p_slot):
        j, i = divmod(n, n_heads)
        kv = slice((i % 2) * HEAD_DIM, (i % 2 + 1) * HEAD_DIM)
        v_t = jnp.concatenate(
            [vp_ref[0, kv] if j == 0 else vc_ref[j - 1, kv], vc_ref[j, kv],
             vn_ref[0, kv] if j == nsub - 1 else vc_ref[j + 1, kv]], axis=1)
        pending[i % 2] = _values(_with_ones(v_t), sink_ref[head_ids[i]], m, p_slot)
        if i % 2 == 1:
            o_ref[j * blk:(j + 1) * blk, (i // 2) * LANES:(i // 2 + 1) * LANES] = (
                jnp.concatenate([pending[0], pending[1]], axis=0).T.astype(_BF))

    _pipelined(nsub * n_heads, score_job, value_job, s_scr, p_scr)


def _win_attn(sink, qa, ka, vta, bias, seq):
    n = qa.shape[0]
    tq = min(WIN_TILE, seq)
    blk = A_WINDOW
    sub = tq // blk
    n_blk = n // blk
    prev_blk = lambda t: jnp.maximum(t * sub - 1, 0)
    next_blk = lambda t: jnp.minimum((t + 1) * sub, n_blk - 1)
    kernel = functools.partial(_win_attn_kernel, tiles_per_batch=seq // tq)
    kw, vw = ka.shape[1], vta.shape[1]
    return pl.pallas_call(
        kernel,
        grid=(n // tq,),
        in_specs=[pl.BlockSpec(memory_space=pltpu.SMEM),
                  pl.BlockSpec((tq, qa.shape[1]), lambda t: (t, 0)),
                  pl.BlockSpec((blk, kw), lambda t: (prev_blk(t), 0)),
                  pl.BlockSpec((tq, kw), lambda t: (t, 0)),
                  pl.BlockSpec((blk, kw), lambda t: (next_blk(t), 0)),
                  pl.BlockSpec((1, vw, CHUNK), lambda t: (prev_blk(t), 0, 0)),
                  pl.BlockSpec((sub, vw, CHUNK), lambda t: (t, 0, 0)),
                  pl.BlockSpec((1, vw, CHUNK), lambda t: (next_blk(t), 0, 0)),
                  _resident(bias.shape)],
        out_specs=pl.BlockSpec((tq, qa.shape[1]), lambda t: (t, 0)),
        out_shape=jax.ShapeDtypeStruct(qa.shape, _BF),
        scratch_shapes=[pltpu.VMEM((PIPE_SLOTS, 3 * blk, LANES), _F32),
                        pltpu.VMEM((PIPE_SLOTS, 3 * blk, LANES), _BF)],
        compiler_params=_params(),
        name="win_attn",
    )(sink, qa, ka, ka, ka, vta, vta, vta, bias)


def _win_bias(n_heads):
    slopes = 2.0 ** (-8.0 * jnp.arange(1, n_heads + 1, dtype=_F32) / n_heads)
    k_pos = jnp.arange(3 * A_WINDOW)[:, None] - A_WINDOW
    q_pos = jnp.arange(A_WINDOW)[None, :]
    dist = jnp.abs(q_pos - k_pos)
    bias = -(LOG2E * slopes)[:, None, None] * dist.astype(_F32)[None]
    return jnp.where((dist <= A_WINDOW)[None], bias, NEG)


def _nbr_attn_kernel(q_ref, kwin_ref, vwin_ref, bias_ref, o_ref, s_scr, p_scr, *, rows, tiles_per_batch):
    rt = q_ref.shape[0] // GRID_W
    win_rows = kwin_ref.shape[0] // GRID_W
    n_pairs = q_ref.shape[1] // LANES
    n_keys = NBR_KEY_ROWS * GRID_W
    n_chunks = n_keys // CHUNK
    first_row = (pl.program_id(0) % tiles_per_batch) * rt
    win_row0 = jnp.clip(first_row - B_KH // 2, 0, rows - win_rows)

    n_heads = 2 * n_pairs

    def key_row0(jp):
        r = first_row + 2 * jp
        return jnp.clip(r - B_KH // 2, 0, rows - NBR_KEY_ROWS) - win_row0

    pending = {}

    def score_job(n, s_slot):
        jp, h = divmod(n, n_heads)
        r = first_row + 2 * jp
        k_off = key_row0(jp) * GRID_W
        combo = jnp.where(r == 0, 1, jnp.where(r == 2, 2, jnp.where(r == rows - 4, 3,
                                                                    jnp.where(r == rows - 2, 4, 0))))
        cols = slice((h // 2) * LANES, (h // 2 + 1) * LANES)
        q_tile = q_ref[jp * 2 * GRID_W:(jp + 1) * 2 * GRID_W, cols]
        return _scores(
            lambda c: kwin_ref[pl.ds(pl.multiple_of(k_off + c * CHUNK, CHUNK), CHUNK), cols],
            _one_head(q_tile, h % 2),
            lambda c: bias_ref[combo, h, c * CHUNK:(c + 1) * CHUNK], None, s_slot)

    def value_job(n, m, p_slot):
        jp, h = divmod(n, n_heads)
        c_off = key_row0(jp) // 2
        v_t = jnp.concatenate([vwin_ref[c_off + c, h * HEAD_DIM:(h + 1) * HEAD_DIM] for c in range(n_chunks)],
                              axis=1)
        pending[h % 2] = _values(_with_ones(v_t), None, m, p_slot)
        if h % 2 == 1:
            o_ref[jp * 2 * GRID_W:(jp + 1) * 2 * GRID_W, (h // 2) * LANES:(h // 2 + 1) * LANES] = (
                jnp.concatenate([pending[0], pending[1]], axis=0).T.astype(_BF))

    _pipelined((rt // 2) * n_heads, score_job, value_job, s_scr, p_scr)


def _nbr_attn(qb, kb, vtb, bias, seq):
    n, width = qb.shape
    rows = seq // GRID_W
    rt = NBR_ROWS
    win_rows = rt + B_KH
    assert rows >= win_rows and rows % rt == 0 and win_rows % 2 == 0
    tile = rt * GRID_W
    tiles_per_batch = seq // tile
    kernel = functools.partial(_nbr_attn_kernel, rows=rows, tiles_per_batch=tiles_per_batch)

    def win_token(t):
        first_row = (t % tiles_per_batch) * rt
        win_row0 = jnp.clip(first_row - B_KH // 2, 0, rows - win_rows)
        return (t // tiles_per_batch) * seq + win_row0 * GRID_W

    return pl.pallas_call(
        kernel,
        grid=(n // tile,),
        in_specs=[pl.BlockSpec((tile, width), lambda t: (t, 0)),
                  pl.BlockSpec((pl.Element(win_rows * GRID_W), pl.Element(width)),
                               lambda t: (pl.multiple_of(win_token(t), CHUNK), 0)),
                  pl.BlockSpec((pl.Element(win_rows * GRID_W // CHUNK), pl.Element(width), pl.Element(CHUNK)),
                               lambda t: (win_token(t) // CHUNK, 0, 0)),
                  _resident(bias.shape)],
        out_specs=pl.BlockSpec((tile, width), lambda t: (t, 0)),
        out_shape=jax.ShapeDtypeStruct(qb.shape, _BF),
        scratch_shapes=[pltpu.VMEM((PIPE_SLOTS, NBR_KEY_ROWS * GRID_W, LANES), _F32),
                        pltpu.VMEM((PIPE_SLOTS, NBR_KEY_ROWS * GRID_W, LANES), _BF)],
        compiler_params=_params(),
        name="nbr_attn",
    )(qb, kb, vtb, bias)


def _nbr_bias(rpb):
    variants = [(-4, 0, -5, 1), (0, 0, -1, 0), (-2, 0, -3, 0), (-6, 2, -7, 2), (-8, 2, -9, 2)]
    n_heads, n_dr, n_dc = rpb.shape
    period = 2 * GRID_W
    lo = GRID_W - (B_KW - 1)
    g = jnp.pad(rpb.astype(_F32), ((0, 0), (0, 0), (lo, period - lo - n_dc)), constant_values=NEG)
    g = g[:, :, ::-1]
    flat = jnp.tile(g, (1, 1, GRID_W))[:, :, :GRID_W * (period - 1)]
    toep = flat.reshape(n_heads, n_dr, GRID_W, period - 1)[..., GRID_W - 1:2 * GRID_W - 1]
    c = np.arange(GRID_W)[None, :]
    kc = np.arange(GRID_W)[:, None]
    col_start = np.clip(c - B_KW // 2, 0, GRID_W - B_KW)
    col_ok = (kc >= col_start) & (kc < col_start + B_KW)
    toep = jnp.where(col_ok[None, None], toep, NEG)

    def one_row(a, b):
        first = a + (B_KH - 1) + b
        t = jnp.pad(toep[:, first:first + B_KH], ((0, 0), (b, NBR_KEY_ROWS - B_KH - b), (0, 0), (0, 0)),
                    constant_values=NEG)
        return t.reshape(n_heads, NBR_KEY_ROWS * GRID_W, GRID_W)

    return jnp.stack([jnp.concatenate([one_row(a1, b1), one_row(a2, b2)], axis=-1)
                      for a1, b1, a2, b2 in variants])


def _out_mlp_kernel(x_ref, ya_ref, yb_ref, gate_ref, wpa_ref, wpb_ref, wo_ref, g2_ref, wup_ref, wdn_ref,
                    gf_ref, o_ref, *, final_norm):
    d = x_ref.shape[1]
    pa = jnp.dot(ya_ref[...], wpa_ref[...], preferred_element_type=_F32)
    pb = jnp.dot(yb_ref[...], wpb_ref[...], preferred_element_type=_F32)
    merged = gate_ref[:, :d].astype(_F32) * pa + gate_ref[:, d:].astype(_F32) * pb
    x1 = x_ref[...] + jnp.dot(merged.astype(_BF), wo_ref[...], preferred_element_type=_F32)
    h2 = _rms(x1, g2_ref[...]).astype(_BF)
    acc = x1
    d_ff = wup_ref.shape[1]
    for c in range(0, d_ff, FF_CHUNK):
        u = jnp.dot(h2, wup_ref[:, c:c + FF_CHUNK], preferred_element_type=_F32)
        a = jnp.square(jnp.maximum(u, 0.0)).astype(_BF)
        acc = acc + jnp.dot(a, wdn_ref[c:c + FF_CHUNK, :], preferred_element_type=_F32)
    o_ref[...] = _rms(acc, gf_ref[...]) if final_norm else acc


def _out_mlp(xf, ya, yb, gates, wpa, wpb, wo, g2, wup, wdn, gf, final_norm):
    n, d = xf.shape
    tm = min(TOKEN_TILE, n)
    row = lambda width: pl.BlockSpec((tm, width), lambda t: (t, 0))
    return pl.pallas_call(
        functools.partial(_out_mlp_kernel, final_norm=final_norm),
        grid=(n // tm,),
        in_specs=[row(d), row(ya.shape[1]), row(yb.shape[1]), row(gates.shape[1]),
                  _resident(wpa.shape), _resident(wpb.shape), _resident(wo.shape), _resident((1, d)),
                  _resident(wup.shape), _resident(wdn.shape), _resident((1, d))],
        out_specs=row(d),
        out_shape=jax.ShapeDtypeStruct((n, d), _F32),
        compiler_params=_params(),
        name="out_mlp",
    )(xf, ya, yb, gates, wpa, wpb, wo, g2, wup, wdn, gf)


def kernel(x, norm_mix, w_in, b_gate, sink, rpb, w_proj_a, w_proj_b, w_out, norm_mlp, w_up, w_down, norm_final):
    b, s, d = x.shape
    depth = w_in.shape[0]
    a_heads = sink.shape[1]
    b_heads = rpb.shape[1]
    a_w, kv_w, b_w = a_heads * HEAD_DIM, A_KV_HEADS * HEAD_DIM, b_heads * HEAD_DIM
    pair_order = np.array([h for p in range(a_heads // 2) for h in (p, p + a_heads // 2)])
    scale = LOG2E * HEAD_DIM ** -0.5
    win_bias = _win_bias(a_heads)
    xf = x.reshape(b * s, d)
    for l in range(depth):
        w = w_in[l]
        ka0, va0, qb0 = a_w, a_w + kv_w, a_w + 2 * kv_w
        kb0, vb0, g0 = qb0 + b_w, qb0 + 2 * b_w, qb0 + 3 * b_w
        wqa = w[:, :a_w].reshape(d, a_heads, HEAD_DIM)[:, pair_order].reshape(d, a_w) * scale
        w_main = jnp.concatenate([wqa, w[:, ka0:va0], w[:, qb0:kb0] * scale, w[:, kb0:vb0], w[:, g0:]], axis=1)
        w_vt = jnp.concatenate([w[:, va0:qb0], w[:, vb0:g0]], axis=1).T
        qa, ka, qb, kb, gates, vta, vtb = _in_proj(
            xf, norm_mix[l][None], w_main.astype(_BF), w_vt.astype(_BF), b_gate[l][None],
            (a_w, kv_w, b_w, b_w, 2 * d), (kv_w, b_w))
        ya = _win_attn(LOG2E * sink[l].astype(_F32), qa, ka, vta, win_bias, s)
        yb = _nbr_attn(qb, kb, vtb, _nbr_bias(LOG2E * rpb[l].astype(_F32)), s)
        wpa = w_proj_a[l].reshape(a_heads, HEAD_DIM, d)[pair_order].reshape(a_w, d)
        xf = _out_mlp(xf, ya, yb, gates, wpa.astype(_BF), w_proj_b[l].astype(_BF), w_out[l].astype(_BF),
                      norm_mlp[l][None], w_up[l].astype(_BF), w_down[l].astype(_BF), norm_final[None],
                      final_norm=(l == depth - 1))
    return xf.reshape(b, s, d)
```

```python
import functools

import numpy as np
import jax
import jax.numpy as jnp
from jax import lax
from jax.experimental import pallas as pl
from jax.experimental.pallas import tpu as pltpu

HEAD_DIM = 64
LANES = 128
CHUNK = 128
A_KV_HEADS = 2
A_WINDOW = 128
GRID_W = 64
B_KH = 8
B_KW = 16
NBR_KEY_ROWS = 10
EPS = 1e-6
NEG = -1e30
LOG2E = 1.4426950408889634
ONES_ROWS = 16
PIPE_SLOTS = 2

TOKEN_TILE = 512
WIN_TILE = 1024
NBR_ROWS = 16
FF_CHUNK = 512
VMEM_LIMIT = 56 * 1024 * 1024

_BF = jnp.bfloat16
_F32 = jnp.float32
_NT = (((1,), (1,)), ((), ()))


def _rms(x, g):
    var = jnp.mean(x * x, axis=-1, keepdims=True)
    return (x * lax.rsqrt(var + EPS)) * g


def _resident(shape):
    zeros = (0,) * len(shape)
    return pl.BlockSpec(shape, lambda t: zeros, pipeline_mode=pl.Buffered(1))


def _params(flags=None):
    return pltpu.CompilerParams(dimension_semantics=("arbitrary",), vmem_limit_bytes=VMEM_LIMIT, flags=flags)


def _in_proj_kernel(x_ref, g_ref, w_ref, wvt_ref, bg_ref, qa_ref, ka_ref, qb_ref, kb_ref, gate_ref,
                    vta_ref, vtb_ref):
    hb = _rms(x_ref[...], g_ref[...]).astype(_BF)
    off = 0
    for ref in (qa_ref, ka_ref, qb_ref, kb_ref):
        width = ref.shape[1]
        ref[...] = jnp.dot(hb, w_ref[:, off:off + width], preferred_element_type=_F32).astype(_BF)
        off += width
    gate_w = gate_ref.shape[1]
    step = 512
    for c in range(0, gate_w, step):
        g = jnp.dot(hb, w_ref[:, off + c:off + c + step], preferred_element_type=_F32)
        gate_ref[:, c:c + step] = jax.nn.sigmoid(g + bg_ref[:, c:c + step]).astype(_BF)
    vt = lax.dot_general(wvt_ref[...], hb, _NT, preferred_element_type=_F32).astype(_BF)
    va = vta_ref.shape[1]
    for c in range(vta_ref.shape[0]):
        vta_ref[c] = vt[:va, c * CHUNK:(c + 1) * CHUNK]
        vtb_ref[c] = vt[va:, c * CHUNK:(c + 1) * CHUNK]


def _in_proj(xf, g, w, wvt, bg, widths, v_widths):
    n, d = xf.shape
    tm = min(TOKEN_TILE, n)
    out_shape = [jax.ShapeDtypeStruct((n, wd), _BF) for wd in widths]
    out_shape += [jax.ShapeDtypeStruct((n // CHUNK, vw, CHUNK), _BF) for vw in v_widths]
    out_specs = [pl.BlockSpec((tm, wd), lambda t: (t, 0)) for wd in widths]
    out_specs += [pl.BlockSpec((tm // CHUNK, vw, CHUNK), lambda t: (t, 0, 0)) for vw in v_widths]
    return pl.pallas_call(
        _in_proj_kernel,
        grid=(n // tm,),
        in_specs=[pl.BlockSpec((tm, d), lambda t: (t, 0)),
                  _resident((1, d)), _resident(w.shape), _resident(wvt.shape), _resident(bg.shape)],
        out_specs=out_specs,
        out_shape=out_shape,
        compiler_params=_params(),
        name="in_proj",
    )(xf, g, w, wvt, bg)


def _with_ones(vt):
    return jnp.concatenate([vt, jnp.ones((ONES_ROWS, vt.shape[1]), vt.dtype)], axis=0)


def _scores(k_chunk, qm, bias_chunk, sink, s_scr):
    mx = None
    n_chunks = s_scr.shape[0] // CHUNK
    k_all = jnp.concatenate([k_chunk(c) for c in range(n_chunks)], axis=0)
    s_all = lax.dot_general(k_all, qm, _NT, preferred_element_type=_F32)
    for c in range(n_chunks):
        s_c = s_all[c * CHUNK:(c + 1) * CHUNK] + bias_chunk(c)
        s_scr[c * CHUNK:(c + 1) * CHUNK] = s_c
        mx = s_c if mx is None else jnp.maximum(mx, s_c)
    m = jnp.max(mx, axis=0, keepdims=True)
    return m if sink is None else jnp.maximum(m, sink)


def _probs(m, s_scr, p_scr):
    for c in range(s_scr.shape[0] // CHUNK):
        rows = slice(c * CHUNK, (c + 1) * CHUNK)
        p_scr[rows] = jnp.exp2(s_scr[rows] - m).astype(_BF)


def _values(vt_aug, sink, m, p_scr):
    oa = jnp.dot(vt_aug, p_scr[...], preferred_element_type=_F32)
    l = oa[HEAD_DIM:HEAD_DIM + 1]
    if sink is not None:
        l = l + jnp.exp2(sink - m)
    return oa[:HEAD_DIM] * (1.0 / l)


def _pipelined(n_jobs, score_job, value_job, s_scr, p_scr):
    n_slots = s_scr.shape[0]
    maxes = {}
    for step in range(n_jobs + 2):
        if step < n_jobs:
            maxes[step] = score_job(step, s_scr.at[step % n_slots])
        if 0 <= step - 2 < n_jobs:
            value_job(step - 2, maxes.pop(step - 2), p_scr.at[(step - 2) % n_slots])
        if 0 <= step - 1 < n_jobs:
            _probs(maxes[step - 1], s_scr.at[(step - 1) % n_slots], p_scr.at[(step - 1) % n_slots])


def _one_head(q_tile, half):
    lane = lax.broadcasted_iota(jnp.int32, q_tile.shape, 1)
    keep = (lane < HEAD_DIM) if half == 0 else (lane >= HEAD_DIM)
    return jnp.where(keep, q_tile, jnp.zeros_like(q_tile))


def _win_attn_kernel(sink_ref, q_ref, kp_ref, kc_ref, kn_ref, vp_ref, vc_ref, vn_ref, bias_ref, o_ref,
                     s_scr, p_scr, *, tiles_per_batch):
    blk = A_WINDOW
    nsub = q_ref.shape[0] // blk
    n_pairs = q_ref.shape[1] // LANES
    i = pl.program_id(0) % tiles_per_batch
    first_neg = jnp.where(i == 0, NEG, 0.0)
    last_neg = jnp.where(i == tiles_per_batch - 1, NEG, 0.0)
    head_ids = [h for p in range(n_pairs) for h in (p, p + n_pairs)]

    n_heads = len(head_ids)
    pending = {}

    def score_job(n, s_slot):
        j, i = divmod(n, n_heads)
        h = head_ids[i]

        def k_chunk(c):
            b = j - 1 + c
            if b < 0:
                return kp_ref[...]
            return kn_ref[...] if b == nsub else kc_ref[b * blk:(b + 1) * blk]

        def bias_chunk(c):
            b = bias_ref[h, c * blk:(c + 1) * blk]
            if j == 0 and c == 0:
                b = b + first_neg
            if j == nsub - 1 and c == 2:
                b = b + last_neg
            return b

        q_tile = q_ref[j * blk:(j + 1) * blk, (i // 2) * LANES:(i // 2 + 1) * LANES]
        return _scores(k_chunk, _one_head(q_tile, i % 2), bias_chunk, sink_ref[h], s_slot)

    def value_job(n, m, p_slot):
        j, i = divmod(n, n_heads)
        kv = slice((i % 2) * HEAD_DIM, (i % 2 + 1) * HEAD_DIM)
        v_t = jnp.concatenate(
            [vp_ref[0, kv] if j == 0 else vc_ref[j - 1, kv], vc_ref[j, kv],
             vn_ref[0, kv] if j == nsub - 1 else vc_ref[j + 1, kv]], axis=1)
        pending[i % 2] = _values(_with_ones(v_t), sink_ref[head_ids[i]], m, p_slot)
        if i % 2 == 1:
            o_ref[j * blk:(j + 1) * blk, (i // 2) * LANES:(i // 2 + 1) * LANES] = (
                jnp.concatenate([pending[0], pending[1]], axis=0).T.astype(_BF))

    _pipelined(nsub * n_heads, score_job, value_job, s_scr, p_scr)


def _win_attn(sink, qa, ka, vta, bias, seq):
    n = qa.shape[0]
    tq = min(WIN_TILE, seq)
    blk = A_WINDOW
    sub = tq // blk
    n_blk = n // blk
    prev_blk = lambda t: jnp.maximum(t * sub - 1, 0)
    next_blk = lambda t: jnp.minimum((t + 1) * sub, n_blk - 1)
    kernel = functools.partial(_win_attn_kernel, tiles_per_batch=seq // tq)
    kw, vw = ka.shape[1], vta.shape[1]
    return pl.pallas_call(
        kernel,
        grid=(n // tq,),
        in_specs=[pl.BlockSpec(memory_space=pltpu.SMEM),
                  pl.BlockSpec((tq, qa.shape[1]), lambda t: (t, 0)),
                  pl.BlockSpec((blk, kw), lambda t: (prev_blk(t), 0)),
                  pl.BlockSpec((tq, kw), lambda t: (t, 0)),
                  pl.BlockSpec((blk, kw), lambda t: (next_blk(t), 0)),
                  pl.BlockSpec((1, vw, CHUNK), lambda t: (prev_blk(t), 0, 0)),
                  pl.BlockSpec((sub, vw, CHUNK), lambda t: (t, 0, 0)),
                  pl.BlockSpec((1, vw, CHUNK), lambda t: (next_blk(t), 0, 0)),
                  _resident(bias.shape)],
        out_specs=pl.BlockSpec((tq, qa.shape[1]), lambda t: (t, 0)),
        out_shape=jax.ShapeDtypeStruct(qa.shape, _BF),
        scratch_shapes=[pltpu.VMEM((PIPE_SLOTS, 3 * blk, LANES), _F32),
                        pltpu.VMEM((PIPE_SLOTS, 3 * blk, LANES), _BF)],
        compiler_params=_params(),
        name="win_attn",
    )(sink, qa, ka, ka, ka, vta, vta, vta, bias)


def _win_bias(n_heads):
    slopes = 2.0 ** (-8.0 * jnp.arange(1, n_heads + 1, dtype=_F32) / n_heads)
    k_pos = jnp.arange(3 * A_WINDOW)[:, None] - A_WINDOW
    q_pos = jnp.arange(A_WINDOW)[None, :]
    dist = jnp.abs(q_pos - k_pos)
    bias = -(LOG2E * slopes)[:, None, None] * dist.astype(_F32)[None]
    return jnp.where((dist <= A_WINDOW)[None], bias, NEG)


def _nbr_attn_kernel(q_ref, kwin_ref, vwin_ref, bias_ref, o_ref, s_scr, p_scr, *, rows, tiles_per_batch):
    rt = q_ref.shape[0] // GRID_W
    win_rows = kwin_ref.shape[0] // GRID_W
    n_pairs = q_ref.shape[1] // LANES
    n_keys = NBR_KEY_ROWS * GRID_W
    n_chunks = n_keys // CHUNK
    first_row = (pl.program_id(0) % tiles_per_batch) * rt
    win_row0 = jnp.clip(first_row - B_KH // 2, 0, rows - win_rows)

    n_heads = 2 * n_pairs

    def key_row0(jp):
        r = first_row + 2 * jp
        return jnp.clip(r - B_KH // 2, 0, rows - NBR_KEY_ROWS) - win_row0

    pending = {}

    def score_job(n, s_slot):
        jp, h = divmod(n, n_heads)
        r = first_row + 2 * jp
        k_off = key_row0(jp) * GRID_W
        combo = jnp.where(r == 0, 1, jnp.where(r == 2, 2, jnp.where(r == rows - 4, 3,
                                                                    jnp.where(r == rows - 2, 4, 0))))
        cols = slice((h // 2) * LANES, (h // 2 + 1) * LANES)
        q_tile = q_ref[jp * 2 * GRID_W:(jp + 1) * 2 * GRID_W, cols]
        return _scores(
            lambda c: kwin_ref[pl.ds(pl.multiple_of(k_off + c * CHUNK, CHUNK), CHUNK), cols],
            _one_head(q_tile, h % 2),
            lambda c: bias_ref[h, combo, c * CHUNK:(c + 1) * CHUNK], None, s_slot)

    def value_job(n, m, p_slot):
        jp, h = divmod(n, n_heads)
        c_off = key_row0(jp) // 2
        v_t = jnp.concatenate([vwin_ref[c_off + c, h * HEAD_DIM:(h + 1) * HEAD_DIM] for c in range(n_chunks)],
                              axis=1)
        pending[h % 2] = _values(_with_ones(v_t), None, m, p_slot)
        if h % 2 == 1:
            o_ref[jp * 2 * GRID_W:(jp + 1) * 2 * GRID_W, (h // 2) * LANES:(h // 2 + 1) * LANES] = (
                jnp.concatenate([pending[0], pending[1]], axis=0).T.astype(_BF))

    _pipelined((rt // 2) * n_heads, score_job, value_job, s_scr, p_scr)


def _nbr_attn(qb, kb, vtb, bias, seq):
    n, width = qb.shape
    rows = seq // GRID_W
    rt = NBR_ROWS
    win_rows = rt + B_KH
    assert rows >= win_rows and rows % rt == 0 and win_rows % 2 == 0
    tile = rt * GRID_W
    tiles_per_batch = seq // tile
    kernel = functools.partial(_nbr_attn_kernel, rows=rows, tiles_per_batch=tiles_per_batch)

    def win_token(t):
        first_row = (t % tiles_per_batch) * rt
        win_row0 = jnp.clip(first_row - B_KH // 2, 0, rows - win_rows)
        return (t // tiles_per_batch) * seq + win_row0 * GRID_W

    return pl.pallas_call(
        kernel,
        grid=(n // tile,),
        in_specs=[pl.BlockSpec((tile, width), lambda t: (t, 0)),
                  pl.BlockSpec((pl.Element(win_rows * GRID_W), pl.Element(width)),
                               lambda t: (pl.multiple_of(win_token(t), CHUNK), 0)),
                  pl.BlockSpec((pl.Element(win_rows * GRID_W // CHUNK), pl.Element(width), pl.Element(CHUNK)),
                               lambda t: (win_token(t) // CHUNK, 0, 0)),
                  _resident(bias.shape)],
        out_specs=pl.BlockSpec((tile, width), lambda t: (t, 0)),
        out_shape=jax.ShapeDtypeStruct(qb.shape, _BF),
        scratch_shapes=[pltpu.VMEM((PIPE_SLOTS, NBR_KEY_ROWS * GRID_W, LANES), _F32),
                        pltpu.VMEM((PIPE_SLOTS, NBR_KEY_ROWS * GRID_W, LANES), _BF)],
        compiler_params=_params(),
        name="nbr_attn",
    )(qb, kb, vtb, bias)


def _nbr_bias(rpb):
    variants = [(-4, 0, -5, 1), (0, 0, -1, 0), (-2, 0, -3, 0), (-6, 2, -7, 2), (-8, 2, -9, 2)]
    n_heads, n_dr, n_dc = rpb.shape
    n_var, n_kk = len(variants), NBR_KEY_ROWS
    pick = np.zeros((n_var, n_kk, 2, n_dr), np.float32)
    row_bad = np.ones((n_var, n_kk, 2), np.float32)
    for v, (a1, b1, a2, b2) in enumerate(variants):
        for rs, (a, b) in enumerate(((a1, b1), (a2, b2))):
            for kk in range(b, b + B_KH):
                pick[v, kk, rs, a + kk + (B_KH - 1)] = 1.0
                row_bad[v, kk, rs] = 0.0
    kc, c = np.meshgrid(np.arange(GRID_W), np.arange(GRID_W), indexing="ij")
    col_start = np.clip(c - B_KW // 2, 0, GRID_W - B_KW)
    col_ok = (kc >= col_start) & (kc < col_start + B_KW)
    n_ch = 2 * n_dc + 3
    spread = np.zeros((n_ch, GRID_W, 2, GRID_W), np.float32)
    for rs in range(2):
        for j in range(n_dc):
            spread[rs * n_dc + j, :, rs, :] = ((kc - c + (B_KW - 1)) == j) & col_ok
        spread[2 * n_dc + rs, :, rs, :] = NEG
        spread[2 * n_dc + 2, :, rs, :] = np.where(col_ok, 0.0, NEG)
    rows = jnp.einsum("vkrd,hdj->hvkrj", pick, rpb.astype(_F32), precision=lax.Precision.HIGHEST)
    extra = np.concatenate([row_bad, np.ones((n_var, n_kk, 1), np.float32)], axis=-1)
    lhs = jnp.concatenate([rows.reshape(n_heads, n_var, n_kk, 2 * n_dc),
                           jnp.broadcast_to(extra, (n_heads,) + extra.shape)], axis=-1)
    table = jnp.einsum("hvkx,xyrc->hvkyrc", lhs, spread, precision=lax.Precision.HIGHEST)
    return table.reshape(n_heads, n_var, n_kk * GRID_W, 2 * GRID_W)


def _out_mlp_kernel(x_ref, ya_ref, yb_ref, gate_ref, wpa_ref, wpb_ref, wo_ref, g2_ref, wup_ref, wdn_ref,
                    gf_ref, o_ref, *, final_norm):
    d = x_ref.shape[1]
    pa = jnp.dot(ya_ref[...], wpa_ref[...], preferred_element_type=_F32)
    pb = jnp.dot(yb_ref[...], wpb_ref[...], preferred_element_type=_F32)
    merged = gate_ref[:, :d].astype(_F32) * pa + gate_ref[:, d:].astype(_F32) * pb
    x1 = x_ref[...] + jnp.dot(merged.astype(_BF), wo_ref[...], preferred_element_type=_F32)
    h2 = _rms(x1, g2_ref[...]).astype(_BF)
    acc = x1
    d_ff = wup_ref.shape[1]
    for c in range(0, d_ff, FF_CHUNK):
        u = jnp.dot(h2, wup_ref[:, c:c + FF_CHUNK], preferred_element_type=_F32)
        a = jnp.square(jnp.maximum(u, 0.0)).astype(_BF)
        acc = acc + jnp.dot(a, wdn_ref[c:c + FF_CHUNK, :], preferred_element_type=_F32)
    o_ref[...] = _rms(acc, gf_ref[...]) if final_norm else acc


def _out_mlp(xf, ya, yb, gates, wpa, wpb, wo, g2, wup, wdn, gf, final_norm):
    n, d = xf.shape
    tm = min(TOKEN_TILE, n)
    row = lambda width: pl.BlockSpec((tm, width), lambda t: (t, 0))
    return pl.pallas_call(
        functools.partial(_out_mlp_kernel, final_norm=final_norm),
        grid=(n // tm,),
        in_specs=[row(d), row(ya.shape[1]), row(yb.shape[1]), row(gates.shape[1]),
                  _resident(wpa.shape), _resident(wpb.shape), _resident(wo.shape), _resident((1, d)),
                  _resident(wup.shape), _resident(wdn.shape), _resident((1, d))],
        out_specs=row(d),
        out_shape=jax.ShapeDtypeStruct((n, d), _F32),
        compiler_params=_params(),
        name="out_mlp",
    )(xf, ya, yb, gates, wpa, wpb, wo, g2, wup, wdn, gf)


def kernel(x, norm_mix, w_in, b_gate, sink, rpb, w_proj_a, w_proj_b, w_out, norm_mlp, w_up, w_down, norm_final):
    b, s, d = x.shape
    depth = w_in.shape[0]
    a_heads = sink.shape[1]
    b_heads = rpb.shape[1]
    a_w, kv_w, b_w = a_heads * HEAD_DIM, A_KV_HEADS * HEAD_DIM, b_heads * HEAD_DIM
    pair_order = np.array([h for p in range(a_heads // 2) for h in (p, p + a_heads // 2)])
    scale = LOG2E * HEAD_DIM ** -0.5
    win_bias = _win_bias(a_heads)
    xf = x.reshape(b * s, d)
    for l in range(depth):
        w = w_in[l]
        ka0, va0, qb0 = a_w, a_w + kv_w, a_w + 2 * kv_w
        kb0, vb0, g0 = qb0 + b_w, qb0 + 2 * b_w, qb0 + 3 * b_w
        wqa = w[:, :a_w].reshape(d, a_heads, HEAD_DIM)[:, pair_order].reshape(d, a_w) * scale
        w_main = jnp.concatenate([wqa, w[:, ka0:va0], w[:, qb0:kb0] * scale, w[:, kb0:vb0], w[:, g0:]], axis=1)
        w_vt = jnp.concatenate([w[:, va0:qb0], w[:, vb0:g0]], axis=1).T
        qa, ka, qb, kb, gates, vta, vtb = _in_proj(
            xf, norm_mix[l][None], w_main.astype(_BF), w_vt.astype(_BF), b_gate[l][None],
            (a_w, kv_w, b_w, b_w, 2 * d), (kv_w, b_w))
        ya = _win_attn(LOG2E * sink[l].astype(_F32), qa, ka, vta, win_bias, s)
        yb = _nbr_attn(qb, kb, vtb, _nbr_bias(LOG2E * rpb[l].astype(_F32)), s)
        wpa = w_proj_a[l].reshape(a_heads, HEAD_DIM, d)[pair_order].reshape(a_w, d)
        xf = _out_mlp(xf, ya, yb, gates, wpa.astype(_BF), w_proj_b[l].astype(_BF), w_out[l].astype(_BF),
                      norm_mlp[l][None], w_up[l].astype(_BF), w_down[l].astype(_BF), norm_final[None],
                      final_norm=(l == depth - 1))
    return xf.reshape(b, s, d)
```

```python
import functools

import numpy as np
import jax
import jax.numpy as jnp
from jax import lax
from jax.experimental import pallas as pl
from jax.experimental.pallas import tpu as pltpu

HEAD_DIM = 64
LANES = 128
CHUNK = 128
A_KV_HEADS = 2
A_WINDOW = 128
GRID_W = 64
B_KH = 8
B_KW = 16
NBR_KEY_ROWS = 10
EPS = 1e-6
NEG = -1e30
LOG2E = 1.4426950408889634
ONES_ROWS = 16
PIPE_SLOTS = 2

TOKEN_TILE = 512
WIN_TILE = 1024
NBR_ROWS = 16
FF_CHUNK = 512
VMEM_LIMIT = 56 * 1024 * 1024

_BF = jnp.bfloat16
_F32 = jnp.float32
_NT = (((1,), (1,)), ((), ()))


def _rms(x, g):
    var = jnp.mean(x * x, axis=-1, keepdims=True)
    return (x * lax.rsqrt(var + EPS)) * g


def _resident(shape):
    zeros = (0,) * len(shape)
    return pl.BlockSpec(shape, lambda t: zeros, pipeline_mode=pl.Buffered(1))


def _params(flags=None):
    return pltpu.CompilerParams(dimension_semantics=("arbitrary",), vmem_limit_bytes=VMEM_LIMIT, flags=flags)


def _in_proj_kernel(x_ref, g_ref, w_ref, wvt_ref, bg_ref, qa_ref, ka_ref, qb_ref, kb_ref, gate_ref,
                    vta_ref, vtb_ref):
    hb = _rms(x_ref[...], g_ref[...]).astype(_BF)
    off = 0
    for ref in (qa_ref, ka_ref, qb_ref, kb_ref):
        width = ref.shape[1]
        ref[...] = jnp.dot(hb, w_ref[:, off:off + width], preferred_element_type=_F32).astype(_BF)
        off += width
    gate_w = gate_ref.shape[1]
    step = 512
    for c in range(0, gate_w, step):
        g = jnp.dot(hb, w_ref[:, off + c:off + c + step], preferred_element_type=_F32)
        gate_ref[:, c:c + step] = jax.nn.sigmoid(g + bg_ref[:, c:c + step]).astype(_BF)
    vt = lax.dot_general(wvt_ref[...], hb, _NT, preferred_element_type=_F32).astype(_BF)
    va = vta_ref.shape[1]
    for c in range(vta_ref.shape[0]):
        vta_ref[c] = vt[:va, c * CHUNK:(c + 1) * CHUNK]
        vtb_ref[c] = vt[va:, c * CHUNK:(c + 1) * CHUNK]


def _in_proj(xf, g, w, wvt, bg, widths, v_widths):
    n, d = xf.shape
    tm = min(TOKEN_TILE, n)
    out_shape = [jax.ShapeDtypeStruct((n, wd), _BF) for wd in widths]
    out_shape += [jax.ShapeDtypeStruct((n // CHUNK, vw, CHUNK), _BF) for vw in v_widths]
    out_specs = [pl.BlockSpec((tm, wd), lambda t: (t, 0)) for wd in widths]
    out_specs += [pl.BlockSpec((tm // CHUNK, vw, CHUNK), lambda t: (t, 0, 0)) for vw in v_widths]
    return pl.pallas_call(
        _in_proj_kernel,
        grid=(n // tm,),
        in_specs=[pl.BlockSpec((tm, d), lambda t: (t, 0)),
                  _resident((1, d)), _resident(w.shape), _resident(wvt.shape), _resident(bg.shape)],
        out_specs=out_specs,
        out_shape=out_shape,
        compiler_params=_params(),
        name="in_proj",
    )(xf, g, w, wvt, bg)


def _with_ones(vt):
    return jnp.concatenate([vt, jnp.ones((ONES_ROWS, vt.shape[1]), vt.dtype)], axis=0)


def _scores(k_chunk, qm, bias_chunk, sink, s_scr):
    mx = None
    n_chunks = s_scr.shape[0] // CHUNK
    k_all = jnp.concatenate([k_chunk(c) for c in range(n_chunks)], axis=0)
    s_all = lax.dot_general(k_all, qm, _NT, preferred_element_type=_F32)
    for c in range(n_chunks):
        s_c = s_all[c * CHUNK:(c + 1) * CHUNK] + bias_chunk(c)
        s_scr[c * CHUNK:(c + 1) * CHUNK] = s_c
        mx = s_c if mx is None else jnp.maximum(mx, s_c)
    m = jnp.max(mx, axis=0, keepdims=True)
    return m if sink is None else jnp.maximum(m, sink)


def _probs(m, s_scr, p_scr):
    for c in range(s_scr.shape[0] // CHUNK):
        rows = slice(c * CHUNK, (c + 1) * CHUNK)
        p_scr[rows] = jnp.exp2(s_scr[rows] - m).astype(_BF)


def _values(vt_aug, sink, m, p_scr):
    oa = jnp.dot(vt_aug, p_scr[...], preferred_element_type=_F32)
    l = oa[HEAD_DIM:HEAD_DIM + 1]
    if sink is not None:
        l = l + jnp.exp2(sink - m)
    return oa[:HEAD_DIM] * (1.0 / l)


def _pipelined(n_jobs, score_job, value_job, s_scr, p_scr):
    n_slots = s_scr.shape[0]
    maxes = {}
    for step in range(n_jobs + 2):
        if step < n_jobs:
            maxes[step] = score_job(step, s_scr.at[step % n_slots])
        if 0 <= step - 2 < n_jobs:
            value_job(step - 2, maxes.pop(step - 2), p_scr.at[(step - 2) % n_slots])
        if 0 <= step - 1 < n_jobs:
            _probs(maxes[step - 1], s_scr.at[(step - 1) % n_slots], p_scr.at[(step - 1) % n_slots])


def _one_head(q_tile, half):
    lane = lax.broadcasted_iota(jnp.int32, q_tile.shape, 1)
    keep = (lane < HEAD_DIM) if half == 0 else (lane >= HEAD_DIM)
    return jnp.where(keep, q_tile, jnp.zeros_like(q_tile))


def _win_attn_kernel(sink_ref, q_ref, kp_ref, kc_ref, kn_ref, vp_ref, vc_ref, vn_ref, bias_ref, o_ref,
                     s_scr, p_scr, *, tiles_per_batch):
    blk = A_WINDOW
    nsub = q_ref.shape[0] // blk
    n_pairs = q_ref.shape[1] // LANES
    i = pl.program_id(0) % tiles_per_batch
    first_neg = jnp.where(i == 0, NEG, 0.0)
    last_neg = jnp.where(i == tiles_per_batch - 1, NEG, 0.0)
    head_ids = [h for p in range(n_pairs) for h in (p, p + n_pairs)]

    n_heads = len(head_ids)
    pending = {}

    def score_job(n, s_slot):
        j, i = divmod(n, n_heads)
        h = head_ids[i]

        def k_chunk(c):
            b = j - 1 + c
            if b < 0:
                return kp_ref[...]
            return kn_ref[...] if b == nsub else kc_ref[b * blk:(b + 1) * blk]

        def bias_chunk(c):
            b = bias_ref[h, c * blk:(c + 1) * blk]
            if j == 0 and c == 0:
                b = b + first_neg
            if j == nsub - 1 and c == 2:
                b = b + last_neg
            return b

        q_tile = q_ref[j * blk:(j + 1) * blk, (i // 2) * LANES:(i // 2 + 1) * LANES]
        return _scores(k_chunk, _one_head(q_tile, i % 2), bias_chunk, sink_ref[h], s_slot)

    def value_job(n, m, p_slot):
        j, i = divmod(n, n_heads)
        kv = slice((i % 2) * HEAD_DIM, (i % 2 + 1) * HEAD_DIM)
        v_t = jnp.concatenate(
            [vp_ref[0, kv] if j == 0 else vc_ref[j - 1, kv], vc_ref[j, kv],
             vn_ref[0, kv] if j == nsub - 1 else vc_ref[j + 1, kv]], axis=1)
        pending[i % 2] = _values(_with_ones(v_t), sink_ref[head_ids[i]], m, p_slot)
        if i % 2 == 1:
            o_ref[j * blk:(j + 1) * blk, (i // 2) * LANES:(i // 2 + 1) * LANES] = (
                jnp.concatenate([pending[0], pending[1]], axis=0).T.astype(_BF))

    _pipelined(nsub * n_heads, score_job, value_job, s_scr, p_scr)


def _win_attn(sink, qa, ka, vta, bias, seq):
    n = qa.shape[0]
    tq = min(WIN_TILE, seq)
    blk = A_WINDOW
    sub = tq // blk
    n_blk = n // blk
    prev_blk = lambda t: jnp.maximum(t * sub - 1, 0)
    next_blk = lambda t: jnp.minimum((t + 1) * sub, n_blk - 1)
    kernel = functools.partial(_win_attn_kernel, tiles_per_batch=seq // tq)
    kw, vw = ka.shape[1], vta.shape[1]
    return pl.pallas_call(
        kernel,
        grid=(n // tq,),
        in_specs=[pl.BlockSpec(memory_space=pltpu.SMEM),
                  pl.BlockSpec((tq, qa.shape[1]), lambda t: (t, 0)),
                  pl.BlockSpec((blk, kw), lambda t: (prev_blk(t), 0)),
                  pl.BlockSpec((tq, kw), lambda t: (t, 0)),
                  pl.BlockSpec((blk, kw), lambda t: (next_blk(t), 0)),
                  pl.BlockSpec((1, vw, CHUNK), lambda t: (prev_blk(t), 0, 0)),
                  pl.BlockSpec((sub, vw, CHUNK), lambda t: (t, 0, 0)),
                  pl.BlockSpec((1, vw, CHUNK), lambda t: (next_blk(t), 0, 0)),
                  _resident(bias.shape)],
        out_specs=pl.BlockSpec((tq, qa.shape[1]), lambda t: (t, 0)),
        out_shape=jax.ShapeDtypeStruct(qa.shape, _BF),
        scratch_shapes=[pltpu.VMEM((PIPE_SLOTS, 3 * blk, LANES), _F32),
                        pltpu.VMEM((PIPE_SLOTS, 3 * blk, LANES), _BF)],
        compiler_params=_params(),
        name="win_attn",
    )(sink, qa, ka, ka, ka, vta, vta, vta, bias)


def _win_bias(n_heads):
    slopes = 2.0 ** (-8.0 * jnp.arange(1, n_heads + 1, dtype=_F32) / n_heads)
    k_pos = jnp.arange(3 * A_WINDOW)[:, None] - A_WINDOW
    q_pos = jnp.arange(A_WINDOW)[None, :]
    dist = jnp.abs(q_pos - k_pos)
    bias = -(LOG2E * slopes)[:, None, None] * dist.astype(_F32)[None]
    return jnp.where((dist <= A_WINDOW)[None], bias, NEG)


def _nbr_attn_kernel(q_ref, kwin_ref, vwin_ref, bias_ref, o_ref, s_scr, p_scr, *, rows, tiles_per_batch):
    rt = q_ref.shape[0] // GRID_W
    win_rows = kwin_ref.shape[0] // GRID_W
    n_pairs = q_ref.shape[1] // LANES
    n_keys = NBR_KEY_ROWS * GRID_W
    n_chunks = n_keys // CHUNK
    first_row = (pl.program_id(0) % tiles_per_batch) * rt
    win_row0 = jnp.clip(first_row - B_KH // 2, 0, rows - win_rows)

    n_heads = 2 * n_pairs

    def key_row0(jp):
        r = first_row + 2 * jp
        return jnp.clip(r - B_KH // 2, 0, rows - NBR_KEY_ROWS) - win_row0

    pending = {}

    def score_job(n, s_slot):
        jp, h = divmod(n, n_heads)
        r = first_row + 2 * jp
        k_off = key_row0(jp) * GRID_W
        combo = jnp.where(r == 0, 1, jnp.where(r == 2, 2, jnp.where(r == rows - 4, 3,
                                                                    jnp.where(r == rows - 2, 4, 0))))
        cols = slice((h // 2) * LANES, (h // 2 + 1) * LANES)
        q_tile = q_ref[jp * 2 * GRID_W:(jp + 1) * 2 * GRID_W, cols]
        return _scores(
            lambda c: kwin_ref[pl.ds(pl.multiple_of(k_off + c * CHUNK, CHUNK), CHUNK), cols],
            _one_head(q_tile, h % 2),
            lambda c: bias_ref[h, combo, c * CHUNK:(c + 1) * CHUNK], None, s_slot)

    def value_job(n, m, p_slot):
        jp, h = divmod(n, n_heads)
        c_off = key_row0(jp) // 2
        v_t = jnp.concatenate([vwin_ref[c_off + c, h * HEAD_DIM:(h + 1) * HEAD_DIM] for c in range(n_chunks)],
                              axis=1)
        pending[h % 2] = _values(_with_ones(v_t), None, m, p_slot)
        if h % 2 == 1:
            o_ref[jp * 2 * GRID_W:(jp + 1) * 2 * GRID_W, (h // 2) * LANES:(h // 2 + 1) * LANES] = (
                jnp.concatenate([pending[0], pending[1]], axis=0).T.astype(_BF))

    _pipelined((rt // 2) * n_heads, score_job, value_job, s_scr, p_scr)


def _nbr_attn(qb, kb, vtb, bias, seq):
    n, width = qb.shape
    rows = seq // GRID_W
    rt = NBR_ROWS
    win_rows = rt + B_KH
    assert rows >= win_rows and rows % rt == 0 and win_rows % 2 == 0
    tile = rt * GRID_W
    tiles_per_batch = seq // tile
    kernel = functools.partial(_nbr_attn_kernel, rows=rows, tiles_per_batch=tiles_per_batch)

    def win_token(t):
        first_row = (t % tiles_per_batch) * rt
        win_row0 = jnp.clip(first_row - B_KH // 2, 0, rows - win_rows)
        return (t // tiles_per_batch) * seq + win_row0 * GRID_W

    return pl.pallas_call(
        kernel,
        grid=(n // tile,),
        in_specs=[pl.BlockSpec((tile, width), lambda t: (t, 0)),
                  pl.BlockSpec((pl.Element(win_rows * GRID_W), pl.Element(width)),
                               lambda t: (pl.multiple_of(win_token(t), CHUNK), 0)),
                  pl.BlockSpec((pl.Element(win_rows * GRID_W // CHUNK), pl.Element(width), pl.Element(CHUNK)),
                               lambda t: (win_token(t) // CHUNK, 0, 0)),
                  _resident(bias.shape)],
        out_specs=pl.BlockSpec((tile, width), lambda t: (t, 0)),
        out_shape=jax.ShapeDtypeStruct(qb.shape, _BF),
        scratch_shapes=[pltpu.VMEM((PIPE_SLOTS, NBR_KEY_ROWS * GRID_W, LANES), _F32),
                        pltpu.VMEM((PIPE_SLOTS, NBR_KEY_ROWS * GRID_W, LANES), _BF)],
        compiler_params=_params(),
        name="nbr_attn",
    )(qb, kb, vtb, bias)


def _nbr_bias(rpb):
    variants = [(-4, 0, -5, 1), (0, 0, -1, 0), (-2, 0, -3, 0), (-6, 2, -7, 2), (-8, 2, -9, 2)]
    n_heads, n_dr, n_dc = rpb.shape
    n_kk = NBR_KEY_ROWS
    assert all(a2 == a1 - 1 for a1, _, a2, _ in variants)
    kc, c = np.meshgrid(np.arange(GRID_W), np.arange(GRID_W), indexing="ij")
    col_start = np.clip(c - B_KW // 2, 0, GRID_W - B_KW)
    col_ok = (kc >= col_start) & (kc < col_start + B_KW)
    spread = np.zeros((2, n_dc, GRID_W, 2, GRID_W), np.float32)
    for rs in range(2):
        for j in range(n_dc):
            spread[rs, j, :, rs, :] = (kc - c + (B_KW - 1)) == j
    spread = spread.reshape(2 * n_dc, GRID_W, 2 * GRID_W)
    n_blocks = max(a1 for a1, *_ in variants) + B_KH + n_kk
    rpb = rpb.astype(_F32)
    lhs = jnp.concatenate([jnp.pad(rpb, ((0, 0), (1 + rs, n_blocks - n_dr - 1 - rs), (0, 0))) for rs in range(2)],
                          axis=-1)
    blocks = jnp.einsum("hix,xkl->hikl", lhs, spread, precision=lax.Precision.HIGHEST)
    slabs = []
    for a1, b1, _, b2 in variants:
        kk = np.arange(n_kk)[:, None, None]
        row_ok = np.concatenate([np.broadcast_to((kk >= b) & (kk < b + B_KH), (n_kk, GRID_W, GRID_W))
                                 for b in (b1, b2)], axis=-1)
        ok = row_ok & np.tile(col_ok, (1, 2))[None]
        slabs.append(jnp.where(ok[None], blocks[:, a1 + B_KH:a1 + B_KH + n_kk], NEG))
    return jnp.stack(slabs, axis=1).reshape(n_heads, len(variants), n_kk * GRID_W, 2 * GRID_W)


def _out_mlp_kernel(x_ref, ya_ref, yb_ref, gate_ref, wpa_ref, wpb_ref, wo_ref, g2_ref, wup_ref, wdn_ref,
                    gf_ref, o_ref, *, final_norm):
    d = x_ref.shape[1]
    pa = jnp.dot(ya_ref[...], wpa_ref[...], preferred_element_type=_F32)
    pb = jnp.dot(yb_ref[...], wpb_ref[...], preferred_element_type=_F32)
    merged = gate_ref[:, :d].astype(_F32) * pa + gate_ref[:, d:].astype(_F32) * pb
    x1 = x_ref[...] + jnp.dot(merged.astype(_BF), wo_ref[...], preferred_element_type=_F32)
    h2 = _rms(x1, g2_ref[...]).astype(_BF)
    acc = x1
    d_ff = wup_ref.shape[1]
    for c in range(0, d_ff, FF_CHUNK):
        u = jnp.dot(h2, wup_ref[:, c:c + FF_CHUNK], preferred_element_type=_F32)
        a = jnp.square(jnp.maximum(u, 0.0)).astype(_BF)
        acc = acc + jnp.dot(a, wdn_ref[c:c + FF_CHUNK, :], preferred_element_type=_F32)
    o_ref[...] = _rms(acc, gf_ref[...]) if final_norm else acc


def _out_mlp(xf, ya, yb, gates, wpa, wpb, wo, g2, wup, wdn, gf, final_norm):
    n, d = xf.shape
    tm = min(TOKEN_TILE, n)
    row = lambda width: pl.BlockSpec((tm, width), lambda t: (t, 0))
    return pl.pallas_call(
        functools.partial(_out_mlp_kernel, final_norm=final_norm),
        grid=(n // tm,),
        in_specs=[row(d), row(ya.shape[1]), row(yb.shape[1]), row(gates.shape[1]),
                  _resident(wpa.shape), _resident(wpb.shape), _resident(wo.shape), _resident((1, d)),
                  _resident(wup.shape), _resident(wdn.shape), _resident((1, d))],
        out_specs=row(d),
        out_shape=jax.ShapeDtypeStruct((n, d), _F32),
        compiler_params=_params(),
        name="out_mlp",
    )(xf, ya, yb, gates, wpa, wpb, wo, g2, wup, wdn, gf)


def kernel(x, norm_mix, w_in, b_gate, sink, rpb, w_proj_a, w_proj_b, w_out, norm_mlp, w_up, w_down, norm_final):
    b, s, d = x.shape
    depth = w_in.shape[0]
    a_heads = sink.shape[1]
    b_heads = rpb.shape[1]
    a_w, kv_w, b_w = a_heads * HEAD_DIM, A_KV_HEADS * HEAD_DIM, b_heads * HEAD_DIM
    pair_order = np.array([h for p in range(a_heads // 2) for h in (p, p + a_heads // 2)])
    scale = LOG2E * HEAD_DIM ** -0.5
    win_bias = _win_bias(a_heads)
    xf = x.reshape(b * s, d)
    for l in range(depth):
        w = w_in[l]
        ka0, va0, qb0 = a_w, a_w + kv_w, a_w + 2 * kv_w
        kb0, vb0, g0 = qb0 + b_w, qb0 + 2 * b_w, qb0 + 3 * b_w
        wqa = w[:, :a_w].reshape(d, a_heads, HEAD_DIM)[:, pair_order].reshape(d, a_w) * scale
        w_main = jnp.concatenate([wqa, w[:, ka0:va0], w[:, qb0:kb0] * scale, w[:, kb0:vb0], w[:, g0:]], axis=1)
        w_vt = jnp.concatenate([w[:, va0:qb0], w[:, vb0:g0]], axis=1).T
        qa, ka, qb, kb, gates, vta, vtb = _in_proj(
            xf, norm_mix[l][None], w_main.astype(_BF), w_vt.astype(_BF), b_gate[l][None],
            (a_w, kv_w, b_w, b_w, 2 * d), (kv_w, b_w))
        ya = _win_attn(LOG2E * sink[l].astype(_F32), qa, ka, vta, win_bias, s)
        yb = _nbr_attn(qb, kb, vtb, _nbr_bias(LOG2E * rpb[l].astype(_F32)), s)
        wpa = w_proj_a[l].reshape(a_heads, HEAD_DIM, d)[pair_order].reshape(a_w, d)
        xf = _out_mlp(xf, ya, yb, gates, wpa.astype(_BF), w_proj_b[l].astype(_BF), w_out[l].astype(_BF),
                      norm_mlp[l][None], w_up[l].astype(_BF), w_down[l].astype(_BF), norm_final[None],
                      final_norm=(l == depth - 1))
    return xf.reshape(b, s, d)
```

```python
import functools

import numpy as np
import jax
import jax.numpy as jnp
from jax import lax
from jax.experimental import pallas as pl
from jax.experimental.pallas import tpu as pltpu

HEAD_DIM = 64
LANES = 128
CHUNK = 128
A_KV_HEADS = 2
A_WINDOW = 128
GRID_W = 64
B_KH = 8
B_KW = 16
NBR_KEY_ROWS = 10
EPS = 1e-6
NEG = -1e30
LOG2E = 1.4426950408889634
ONES_ROWS = 16

IN_TILE = 1024
TOKEN_TILE = 512
WIN_TILE = 1024
NBR_ROWS = 16
FF_CHUNK = 512
VMEM_LIMIT = 56 * 1024 * 1024

_BF = jnp.bfloat16
_F32 = jnp.float32
_NT = (((1,), (1,)), ((), ()))


def _rms(x, g):
    var = jnp.mean(x * x, axis=-1, keepdims=True)
    return (x * lax.rsqrt(var + EPS)) * g


def _resident(shape):
    zeros = (0,) * len(shape)
    return pl.BlockSpec(shape, lambda t: zeros, pipeline_mode=pl.Buffered(1))


def _params():
    return pltpu.CompilerParams(dimension_semantics=("arbitrary",), vmem_limit_bytes=VMEM_LIMIT)


def _in_proj_kernel(x_ref, g_ref, w_ref, wqa_ref, wvt_ref, bg_ref, qa_ref, ka_ref, qb_ref, kb_ref, gate_ref,
                    vta_ref, vtb_ref, *, col_of):
    hb = _rms(x_ref[...], g_ref[...]).astype(_BF)
    qa_ref[...] = jnp.dot(hb, wqa_ref[...], preferred_element_type=_F32).astype(_BF)
    for ref, off in ((ka_ref, col_of["ka"]), (qb_ref, col_of["qb"]), (kb_ref, col_of["kb"])):
        width = ref.shape[1]
        ref[...] = jnp.dot(hb, w_ref[:, off:off + width], preferred_element_type=_F32).astype(_BF)
    off = col_of["gate"]
    gate_w = gate_ref.shape[1]
    step = 512
    for c in range(0, gate_w, step):
        g = jnp.dot(hb, w_ref[:, off + c:off + c + step], preferred_element_type=_F32)
        gate_ref[:, c:c + step] = jax.nn.sigmoid(g + bg_ref[:, c:c + step]).astype(_BF)
    vt = lax.dot_general(wvt_ref[...], hb, _NT, preferred_element_type=_F32).astype(_BF)
    va = vta_ref.shape[1]
    for c in range(vta_ref.shape[0]):
        vta_ref[c] = vt[:va, c * CHUNK:(c + 1) * CHUNK]
        vtb_ref[c] = vt[va:, c * CHUNK:(c + 1) * CHUNK]


def _in_proj(xf, g, w, wqa, wvt, bg, col_of, widths, v_widths):
    n, d = xf.shape
    tm = min(IN_TILE, n)
    out_shape = [jax.ShapeDtypeStruct((n, wd), _BF) for wd in widths]
    out_shape += [jax.ShapeDtypeStruct((n // CHUNK, vw, CHUNK), _BF) for vw in v_widths]
    out_specs = [pl.BlockSpec((tm, wd), lambda t: (t, 0)) for wd in widths]
    out_specs += [pl.BlockSpec((tm // CHUNK, vw, CHUNK), lambda t: (t, 0, 0)) for vw in v_widths]
    return pl.pallas_call(
        functools.partial(_in_proj_kernel, col_of=col_of),
        grid=(n // tm,),
        in_specs=[pl.BlockSpec((tm, d), lambda t: (t, 0)),
                  _resident((1, d)), _resident(w.shape), _resident(wqa.shape), _resident(wvt.shape),
                  _resident(bg.shape)],
        out_specs=out_specs,
        out_shape=out_shape,
        compiler_params=_params(),
        name="in_proj",
    )(xf, g, w, wqa, wvt, bg)


def _with_ones(vt):
    return jnp.concatenate([vt, jnp.ones((ONES_ROWS, vt.shape[1]), vt.dtype)], axis=0)


def _scores(k_all, qm, bias_chunk, sink):
    s_all = lax.dot_general(k_all, qm, _NT, preferred_element_type=_F32)
    chunks = [s_all[c * CHUNK:(c + 1) * CHUNK] + bias_chunk(c) for c in range(k_all.shape[0] // CHUNK)]
    m = jnp.max(functools.reduce(jnp.maximum, chunks), axis=0, keepdims=True)
    return chunks, (m if sink is None else jnp.maximum(m, sink))


def _probs(chunks, m):
    return jnp.concatenate([jnp.exp2(s_c - m).astype(_BF) for s_c in chunks], axis=0)


def _values(vt_aug, sink, m, p):
    oa = jnp.dot(vt_aug, p, preferred_element_type=_F32)
    l = oa[HEAD_DIM:HEAD_DIM + 1]
    if sink is not None:
        l = l + jnp.exp2(sink - m)
    return oa[:HEAD_DIM] * (1.0 / l)


def _pipelined(n_jobs, score_job, value_job):
    scored, probs = {}, {}
    for step in range(n_jobs + 2):
        if step < n_jobs:
            scored[step] = score_job(step)
        if 0 <= step - 2 < n_jobs:
            value_job(step - 2, *probs.pop(step - 2))
        if 0 <= step - 1 < n_jobs:
            chunks, m = scored.pop(step - 1)
            probs[step - 1] = (m, _probs(chunks, m))


def _one_head(q_tile, half):
    lane = lax.broadcasted_iota(jnp.int32, q_tile.shape, 1)
    keep = (lane < HEAD_DIM) if half == 0 else (lane >= HEAD_DIM)
    return jnp.where(keep, q_tile, jnp.zeros_like(q_tile))


def _win_attn_kernel(sink_ref, q_ref, kp_ref, kc_ref, kn_ref, vp_ref, vc_ref, vn_ref, bias_ref, o_ref,
                     *, tiles_per_batch):
    blk = A_WINDOW
    nsub = q_ref.shape[0] // blk
    n_pairs = q_ref.shape[1] // LANES
    i = pl.program_id(0) % tiles_per_batch
    first_neg = jnp.where(i == 0, NEG, 0.0)
    last_neg = jnp.where(i == tiles_per_batch - 1, NEG, 0.0)
    head_ids = [h for p in range(n_pairs) for h in (p, p + n_pairs)]
    n_heads = len(head_ids)
    pending = {}

    def key_block(b):
        if b < 0:
            return kp_ref[...]
        return kn_ref[...] if b == nsub else kc_ref[b * blk:(b + 1) * blk]

    def score_job(n):
        j, i = divmod(n, n_heads)
        h = head_ids[i]

        def bias_chunk(c):
            b = bias_ref[h, c * blk:(c + 1) * blk]
            if j == 0 and c == 0:
                b = b + first_neg
            if j == nsub - 1 and c == 2:
                b = b + last_neg
            return b

        q_tile = q_ref[j * blk:(j + 1) * blk, (i // 2) * LANES:(i // 2 + 1) * LANES]
        k_all = jnp.concatenate([key_block(j - 1 + c) for c in range(3)], axis=0)
        return _scores(k_all, _one_head(q_tile, i % 2), bias_chunk, sink_ref[h])

    def value_job(n, m, p):
        j, i = divmod(n, n_heads)
        kv = slice((i % 2) * HEAD_DIM, (i % 2 + 1) * HEAD_DIM)
        v_t = jnp.concatenate(
            [vp_ref[0, kv] if j == 0 else vc_ref[j - 1, kv], vc_ref[j, kv],
             vn_ref[0, kv] if j == nsub - 1 else vc_ref[j + 1, kv]], axis=1)
        pending[i % 2] = _values(_with_ones(v_t), sink_ref[head_ids[i]], m, p)
        if i % 2 == 1:
            o_ref[j * blk:(j + 1) * blk, (i // 2) * LANES:(i // 2 + 1) * LANES] = (
                jnp.concatenate([pending[0], pending[1]], axis=0).T.astype(_BF))

    _pipelined(nsub * n_heads, score_job, value_job)


def _win_attn(sink, qa, ka, vta, bias, seq):
    n = qa.shape[0]
    tq = min(WIN_TILE, seq)
    blk = A_WINDOW
    sub = tq // blk
    n_blk = n // blk
    prev_blk = lambda t: jnp.maximum(t * sub - 1, 0)
    next_blk = lambda t: jnp.minimum((t + 1) * sub, n_blk - 1)
    kernel = functools.partial(_win_attn_kernel, tiles_per_batch=seq // tq)
    kw, vw = ka.shape[1], vta.shape[1]
    return pl.pallas_call(
        kernel,
        grid=(n // tq,),
        in_specs=[pl.BlockSpec(memory_space=pltpu.SMEM),
                  pl.BlockSpec((tq, qa.shape[1]), lambda t: (t, 0)),
                  pl.BlockSpec((blk, kw), lambda t: (prev_blk(t), 0)),
                  pl.BlockSpec((tq, kw), lambda t: (t, 0)),
                  pl.BlockSpec((blk, kw), lambda t: (next_blk(t), 0)),
                  pl.BlockSpec((1, vw, CHUNK), lambda t: (prev_blk(t), 0, 0)),
                  pl.BlockSpec((sub, vw, CHUNK), lambda t: (t, 0, 0)),
                  pl.BlockSpec((1, vw, CHUNK), lambda t: (next_blk(t), 0, 0)),
                  _resident(bias.shape)],
        out_specs=pl.BlockSpec((tq, qa.shape[1]), lambda t: (t, 0)),
        out_shape=jax.ShapeDtypeStruct(qa.shape, _BF),
        compiler_params=_params(),
        name="win_attn",
    )(sink, qa, ka, ka, ka, vta, vta, vta, bias)


def _win_bias(n_heads):
    slopes = 2.0 ** (-8.0 * jnp.arange(1, n_heads + 1, dtype=_F32) / n_heads)
    k_pos = jnp.arange(3 * A_WINDOW)[:, None] - A_WINDOW
    q_pos = jnp.arange(A_WINDOW)[None, :]
    dist = jnp.abs(q_pos - k_pos)
    bias = -(LOG2E * slopes)[:, None, None] * dist.astype(_F32)[None]
    return jnp.where((dist <= A_WINDOW)[None], bias, NEG)


def _nbr_attn_kernel(q_ref, kwin_ref, vwin_ref, bias_ref, o_ref, *, rows, tiles_per_batch):
    rt = q_ref.shape[0] // GRID_W
    win_rows = kwin_ref.shape[0] // GRID_W
    n_heads = 2 * (q_ref.shape[1] // LANES)
    n_keys = NBR_KEY_ROWS * GRID_W
    n_chunks = n_keys // CHUNK
    first_row = (pl.program_id(0) % tiles_per_batch) * rt
    win_row0 = jnp.clip(first_row - B_KH // 2, 0, rows - win_rows)
    pending = {}

    def key_row0(jp):
        r = first_row + 2 * jp
        return jnp.clip(r - B_KH // 2, 0, rows - NBR_KEY_ROWS) - win_row0

    def score_job(n):
        jp, h = divmod(n, n_heads)
        r = first_row + 2 * jp
        k_off = pl.multiple_of(key_row0(jp) * GRID_W, CHUNK)
        combo = jnp.where(r == 0, 1, jnp.where(r == 2, 2, jnp.where(r == rows - 4, 3,
                                                                    jnp.where(r == rows - 2, 4, 0))))
        cols = slice((h // 2) * LANES, (h // 2 + 1) * LANES)
        q_tile = q_ref[jp * 2 * GRID_W:(jp + 1) * 2 * GRID_W, cols]
        return _scores(kwin_ref[pl.ds(k_off, n_keys), cols], _one_head(q_tile, h % 2),
                       lambda c: bias_ref[h, combo, c * CHUNK:(c + 1) * CHUNK], None)

    def value_job(n, m, p):
        jp, h = divmod(n, n_heads)
        c_off = key_row0(jp) // 2
        v_t = jnp.concatenate([vwin_ref[c_off + c, h * HEAD_DIM:(h + 1) * HEAD_DIM] for c in range(n_chunks)],
                              axis=1)
        pending[h % 2] = _values(_with_ones(v_t), None, m, p)
        if h % 2 == 1:
            o_ref[jp * 2 * GRID_W:(jp + 1) * 2 * GRID_W, (h // 2) * LANES:(h // 2 + 1) * LANES] = (
                jnp.concatenate([pending[0], pending[1]], axis=0).T.astype(_BF))

    _pipelined((rt // 2) * n_heads, score_job, value_job)


def _nbr_attn(qb, kb, vtb, bias, seq):
    n, width = qb.shape
    rows = seq // GRID_W
    rt = NBR_ROWS
    win_rows = rt + B_KH
    assert rows >= win_rows and rows % rt == 0 and win_rows % 2 == 0
    tile = rt * GRID_W
    tiles_per_batch = seq // tile
    kernel = functools.partial(_nbr_attn_kernel, rows=rows, tiles_per_batch=tiles_per_batch)

    def win_token(t):
        first_row = (t % tiles_per_batch) * rt
        win_row0 = jnp.clip(first_row - B_KH // 2, 0, rows - win_rows)
        return (t // tiles_per_batch) * seq + win_row0 * GRID_W

    return pl.pallas_call(
        kernel,
        grid=(n // tile,),
        in_specs=[pl.BlockSpec((tile, width), lambda t: (t, 0)),
                  pl.BlockSpec((pl.Element(win_rows * GRID_W), pl.Element(width)),
                               lambda t: (pl.multiple_of(win_token(t), CHUNK), 0)),
                  pl.BlockSpec((pl.Element(win_rows * GRID_W // CHUNK), pl.Element(width), pl.Element(CHUNK)),
                               lambda t: (win_token(t) // CHUNK, 0, 0)),
                  _resident(bias.shape)],
        out_specs=pl.BlockSpec((tile, width), lambda t: (t, 0)),
        out_shape=jax.ShapeDtypeStruct(qb.shape, _BF),
        compiler_params=_params(),
        name="nbr_attn",
    )(qb, kb, vtb, bias)


def _nbr_bias(rpb):
    variants = [(-4, 0, -5, 1), (0, 0, -1, 0), (-2, 0, -3, 0), (-6, 2, -7, 2), (-8, 2, -9, 2)]
    n_heads, n_dr, n_dc = rpb.shape
    n_kk = NBR_KEY_ROWS
    assert all(a2 == a1 - 1 for a1, _, a2, _ in variants)
    kc, c = np.meshgrid(np.arange(GRID_W), np.arange(GRID_W), indexing="ij")
    col_start = np.clip(c - B_KW // 2, 0, GRID_W - B_KW)
    col_ok = (kc >= col_start) & (kc < col_start + B_KW)
    spread = np.zeros((2, n_dc, GRID_W, 2, GRID_W), np.float32)
    for rs in range(2):
        for j in range(n_dc):
            spread[rs, j, :, rs, :] = (kc - c + (B_KW - 1)) == j
    spread = spread.reshape(2 * n_dc, GRID_W, 2 * GRID_W)
    n_blocks = max(a1 for a1, *_ in variants) + B_KH + n_kk
    rpb = rpb.astype(_F32)
    lhs = jnp.concatenate([jnp.pad(rpb, ((0, 0), (1 + rs, n_blocks - n_dr - 1 - rs), (0, 0))) for rs in range(2)],
                          axis=-1)
    blocks = jnp.einsum("hix,xkl->hikl", lhs, spread, precision=lax.Precision.HIGHEST)
    slabs = []
    for a1, b1, _, b2 in variants:
        kk = np.arange(n_kk)[:, None, None]
        row_ok = np.concatenate([np.broadcast_to((kk >= b) & (kk < b + B_KH), (n_kk, GRID_W, GRID_W))
                                 for b in (b1, b2)], axis=-1)
        ok = row_ok & np.tile(col_ok, (1, 2))[None]
        slabs.append(jnp.where(ok[None], blocks[:, a1 + B_KH:a1 + B_KH + n_kk], NEG))
    return jnp.stack(slabs, axis=1).reshape(n_heads, len(variants), n_kk * GRID_W, 2 * GRID_W)


def _out_mlp_kernel(x_ref, ya_ref, yb_ref, gate_ref, wpa_ref, wpb_ref, wo_ref, g2_ref, wup_ref, wdn_ref,
                    gf_ref, o_ref, *, final_norm):
    d = x_ref.shape[1]
    pa = jnp.dot(ya_ref[...], wpa_ref[...], preferred_element_type=_F32)
    pb = jnp.dot(yb_ref[...], wpb_ref[...], preferred_element_type=_F32)
    merged = gate_ref[:, :d].astype(_F32) * pa + gate_ref[:, d:].astype(_F32) * pb
    x1 = x_ref[...] + jnp.dot(merged.astype(_BF), wo_ref[...], preferred_element_type=_F32)
    h2 = _rms(x1, g2_ref[...]).astype(_BF)
    acc = x1
    d_ff = wup_ref.shape[1]
    for c in range(0, d_ff, FF_CHUNK):
        u = jnp.dot(h2, wup_ref[:, c:c + FF_CHUNK], preferred_element_type=_F32)
        a = jnp.square(jnp.maximum(u, 0.0)).astype(_BF)
        acc = acc + jnp.dot(a, wdn_ref[c:c + FF_CHUNK, :], preferred_element_type=_F32)
    o_ref[...] = _rms(acc, gf_ref[...]) if final_norm else acc


def _out_mlp(xf, ya, yb, gates, wpa, wpb, wo, g2, wup, wdn, gf, final_norm):
    n, d = xf.shape
    tm = min(TOKEN_TILE, n)
    row = lambda width: pl.BlockSpec((tm, width), lambda t: (t, 0))
    return pl.pallas_call(
        functools.partial(_out_mlp_kernel, final_norm=final_norm),
        grid=(n // tm,),
        in_specs=[row(d), row(ya.shape[1]), row(yb.shape[1]), row(gates.shape[1]),
                  _resident(wpa.shape), _resident(wpb.shape), _resident(wo.shape), _resident((1, d)),
                  _resident(wup.shape), _resident(wdn.shape), _resident((1, d))],
        out_specs=row(d),
        out_shape=jax.ShapeDtypeStruct((n, d), _F32),
        compiler_params=_params(),
        name="out_mlp",
    )(xf, ya, yb, gates, wpa, wpb, wo, g2, wup, wdn, gf)


def kernel(x, norm_mix, w_in, b_gate, sink, rpb, w_proj_a, w_proj_b, w_out, norm_mlp, w_up, w_down, norm_final):
    b, s, d = x.shape
    depth = w_in.shape[0]
    a_heads = sink.shape[1]
    b_heads = rpb.shape[1]
    a_w, kv_w, b_w = a_heads * HEAD_DIM, A_KV_HEADS * HEAD_DIM, b_heads * HEAD_DIM
    grp = a_heads // A_KV_HEADS
    scale = LOG2E * HEAD_DIM ** -0.5
    win_bias = _win_bias(a_heads)
    xf = x.reshape(b * s, d)
    for l in range(depth):
        w = w_in[l]
        ka0, va0, qb0 = a_w, a_w + kv_w, a_w + 2 * kv_w
        kb0, vb0, g0 = qb0 + b_w, qb0 + 2 * b_w, qb0 + 3 * b_w
        col_scale = np.ones((1, w.shape[1]), np.float32)
        col_scale[:, :a_w] = scale
        col_scale[:, qb0:kb0] = scale
        w_bf = lax.optimization_barrier((w * col_scale).astype(_BF))
        wqa = w_bf[:, :a_w].reshape(d, A_KV_HEADS, grp, HEAD_DIM).transpose(0, 2, 1, 3).reshape(d, a_w)
        w_vt = jnp.concatenate([w_bf[:, va0:qb0], w_bf[:, vb0:g0]], axis=1).T
        qa, ka, qb, kb, gates, vta, vtb = _in_proj(
            xf, norm_mix[l][None], w_bf, wqa, w_vt,
            b_gate[l][None], {"ka": ka0, "qb": qb0, "kb": kb0, "gate": g0},
            (a_w, kv_w, b_w, b_w, 2 * d), (kv_w, b_w))
        ya = _win_attn(LOG2E * sink[l].astype(_F32), qa, ka, vta, win_bias, s)
        yb = _nbr_attn(qb, kb, vtb, _nbr_bias(LOG2E * rpb[l].astype(_F32)), s)
        wpa = w_proj_a[l].reshape(A_KV_HEADS, grp, HEAD_DIM, d).transpose(1, 0, 2, 3).reshape(a_w, d)
        xf = _out_mlp(xf, ya, yb, gates, wpa.astype(_BF), w_proj_b[l].astype(_BF), w_out[l].astype(_BF),
                      norm_mlp[l][None], w_up[l].astype(_BF), w_down[l].astype(_BF), norm_final[None],
                      final_norm=(l == depth - 1))
    return xf.reshape(b, s, d)
```

```python
import functools

import numpy as np
import jax
import jax.numpy as jnp
from jax import lax
from jax.experimental import pallas as pl
from jax.experimental.pallas import tpu as pltpu

HEAD_DIM = 64
LANES = 128
CHUNK = 128
A_KV_HEADS = 2
A_WINDOW = 128
GRID_W = 64
B_KH = 8
B_KW = 16
NBR_KEY_ROWS = 10
EPS = 1e-6
NEG = -1e30
LOG2E = 1.4426950408889634
ONES_ROWS = 16

IN_TILE = 1024
TOKEN_TILE = 512
WIN_TILE = 2048
NBR_ROWS = 32
FF_CHUNK = 512
VMEM_LIMIT = 56 * 1024 * 1024

_BF = jnp.bfloat16
_F32 = jnp.float32
_NT = (((1,), (1,)), ((), ()))


def _rms(x, g):
    var = jnp.mean(x * x, axis=-1, keepdims=True)
    return (x * lax.rsqrt(var + EPS)) * g


def _resident(shape):
    zeros = (0,) * len(shape)
    return pl.BlockSpec(shape, lambda t: zeros, pipeline_mode=pl.Buffered(1))


def _params():
    return pltpu.CompilerParams(dimension_semantics=("arbitrary",), vmem_limit_bytes=VMEM_LIMIT)


def _in_proj_kernel(x_ref, g_ref, w_ref, bg_ref, qa_ref, ka_ref, qb_ref, kb_ref, gate_ref, vta_ref, vtb_ref,
                    *, col_of):
    hb = _rms(x_ref[...], g_ref[...]).astype(_BF)

    def proj(name, width):
        off = col_of[name]
        return jnp.dot(hb, w_ref[:, off:off + width], preferred_element_type=_F32)

    qa_ref[...] = proj("qa", qa_ref.shape[1]).astype(_BF)
    qb_ref[...] = proj("qb", qb_ref.shape[1]).astype(_BF)
    kb_ref[...] = proj("kb", kb_ref.shape[1]).astype(_BF)
    kv = proj("ka", LANES + vta_ref.shape[1])
    k, v_a = kv[:, :LANES], kv[:, LANES:]
    k_swapped = pltpu.roll(k, HEAD_DIM, axis=1)
    lo = lax.broadcasted_iota(jnp.int32, k.shape, 1) < HEAD_DIM
    ka_ref[:, :LANES] = jnp.where(lo, k, k_swapped).astype(_BF)
    ka_ref[:, LANES:] = jnp.where(lo, k_swapped, k).astype(_BF)
    gate_w = gate_ref.shape[1]
    step = 512
    for c in range(0, gate_w, step):
        g = jnp.dot(hb, w_ref[:, col_of["gate"] + c:col_of["gate"] + c + step], preferred_element_type=_F32)
        gate_ref[:, c:c + step] = jax.nn.sigmoid(g + bg_ref[:, c:c + step]).astype(_BF)
    for v, ref in ((v_a, vta_ref), (proj("vb", vtb_ref.shape[1]), vtb_ref)):
        vt = v.T.astype(_BF)
        for c in range(ref.shape[0]):
            ref[c] = vt[:, c * CHUNK:(c + 1) * CHUNK]


def _in_proj(xf, g, w, bg, col_of, widths, v_widths):
    n, d = xf.shape
    tm = min(IN_TILE, n)
    out_shape = [jax.ShapeDtypeStruct((n, wd), _BF) for wd in widths]
    out_shape += [jax.ShapeDtypeStruct((n // CHUNK, vw, CHUNK), _BF) for vw in v_widths]
    out_specs = [pl.BlockSpec((tm, wd), lambda t: (t, 0)) for wd in widths]
    out_specs += [pl.BlockSpec((tm // CHUNK, vw, CHUNK), lambda t: (t, 0, 0)) for vw in v_widths]
    return pl.pallas_call(
        functools.partial(_in_proj_kernel, col_of=col_of),
        grid=(n // tm,),
        in_specs=[pl.BlockSpec((tm, d), lambda t: (t, 0)),
                  _resident((1, d)), _resident(w.shape), _resident(bg.shape)],
        out_specs=out_specs,
        out_shape=out_shape,
        compiler_params=_params(),
        name="in_proj",
    )(xf, g, w, bg)


def _with_ones(vt):
    return jnp.concatenate([vt, jnp.ones((ONES_ROWS, vt.shape[1]), vt.dtype)], axis=0)


def _scores(k_all, qm, bias_chunk, sink):
    s_all = lax.dot_general(k_all, qm, _NT, preferred_element_type=_F32)
    chunks = [s_all[c * CHUNK:(c + 1) * CHUNK] + bias_chunk(c) for c in range(k_all.shape[0] // CHUNK)]
    m = jnp.max(functools.reduce(jnp.maximum, chunks), axis=0, keepdims=True)
    return chunks, (m if sink is None else jnp.maximum(m, sink))


def _probs(chunks, m):
    return jnp.concatenate([jnp.exp2(s_c - m).astype(_BF) for s_c in chunks], axis=0)


def _values(vt_aug, sink, m, p):
    oa = jnp.dot(vt_aug, p, preferred_element_type=_F32)
    l = oa[HEAD_DIM:HEAD_DIM + 1]
    if sink is not None:
        l = l + jnp.exp2(sink - m)
    return oa[:HEAD_DIM] * (1.0 / l)


def _pipelined(n_jobs, score_job, value_job):
    scored, probs = {}, {}
    for step in range(n_jobs + 2):
        if step < n_jobs:
            scored[step] = score_job(step)
        if 0 <= step - 2 < n_jobs:
            value_job(step - 2, *probs.pop(step - 2))
        if 0 <= step - 1 < n_jobs:
            chunks, m = scored.pop(step - 1)
            probs[step - 1] = (m, _probs(chunks, m))


def _one_head(q_tile, half):
    lane = lax.broadcasted_iota(jnp.int32, q_tile.shape, 1)
    keep = (lane < HEAD_DIM) if half == 0 else (lane >= HEAD_DIM)
    return jnp.where(keep, q_tile, jnp.zeros_like(q_tile))


def _win_attn_kernel(sink_ref, q_ref, kp_ref, kc_ref, kn_ref, vp_ref, vc_ref, vn_ref, bias_ref, o_ref,
                     *, tiles_per_batch):
    blk = A_WINDOW
    nsub = q_ref.shape[0] // blk
    n_pairs = q_ref.shape[1] // LANES
    i = pl.program_id(0) % tiles_per_batch
    first_neg = jnp.where(i == 0, NEG, 0.0)
    last_neg = jnp.where(i == tiles_per_batch - 1, NEG, 0.0)
    n_heads = 2 * n_pairs
    grp = n_heads // A_KV_HEADS
    pending = {}

    def key_block(b, kv):
        cols = slice(kv * LANES, (kv + 1) * LANES)
        if b < 0:
            return kp_ref[:, cols]
        return kn_ref[:, cols] if b == nsub else kc_ref[b * blk:(b + 1) * blk, cols]

    def score_job(n):
        j, h = divmod(n, n_heads)

        def bias_chunk(c):
            b = bias_ref[h, c * blk:(c + 1) * blk]
            if j == 0 and c == 0:
                b = b + first_neg
            if j == nsub - 1 and c == 2:
                b = b + last_neg
            return b

        q_tile = q_ref[j * blk:(j + 1) * blk, (h // 2) * LANES:(h // 2 + 1) * LANES]
        k_all = jnp.concatenate([key_block(j - 1 + c, h // grp) for c in range(3)], axis=0)
        return _scores(k_all, _one_head(q_tile, h % 2), bias_chunk, sink_ref[h])

    def value_job(n, m, p):
        j, h = divmod(n, n_heads)
        kv = slice((h // grp) * HEAD_DIM, (h // grp + 1) * HEAD_DIM)
        v_t = jnp.concatenate(
            [vp_ref[0, kv] if j == 0 else vc_ref[j - 1, kv], vc_ref[j, kv],
             vn_ref[0, kv] if j == nsub - 1 else vc_ref[j + 1, kv]], axis=1)
        pending[h % 2] = _values(_with_ones(v_t), sink_ref[h], m, p)
        if h % 2 == 1:
            o_ref[j * blk:(j + 1) * blk, (h // 2) * LANES:(h // 2 + 1) * LANES] = (
                jnp.concatenate([pending[0], pending[1]], axis=0).T.astype(_BF))

    _pipelined(nsub * n_heads, score_job, value_job)


def _win_attn(sink, qa, ka, vta, bias, seq):
    n = qa.shape[0]
    tq = min(WIN_TILE, seq)
    blk = A_WINDOW
    sub = tq // blk
    n_blk = n // blk
    prev_blk = lambda t: jnp.maximum(t * sub - 1, 0)
    next_blk = lambda t: jnp.minimum((t + 1) * sub, n_blk - 1)
    kernel = functools.partial(_win_attn_kernel, tiles_per_batch=seq // tq)
    kw, vw = ka.shape[1], vta.shape[1]
    return pl.pallas_call(
        kernel,
        grid=(n // tq,),
        in_specs=[pl.BlockSpec(memory_space=pltpu.SMEM),
                  pl.BlockSpec((tq, qa.shape[1]), lambda t: (t, 0)),
                  pl.BlockSpec((blk, kw), lambda t: (prev_blk(t), 0)),
                  pl.BlockSpec((tq, kw), lambda t: (t, 0)),
                  pl.BlockSpec((blk, kw), lambda t: (next_blk(t), 0)),
                  pl.BlockSpec((1, vw, CHUNK), lambda t: (prev_blk(t), 0, 0)),
                  pl.BlockSpec((sub, vw, CHUNK), lambda t: (t, 0, 0)),
                  pl.BlockSpec((1, vw, CHUNK), lambda t: (next_blk(t), 0, 0)),
                  _resident(bias.shape)],
        out_specs=pl.BlockSpec((tq, qa.shape[1]), lambda t: (t, 0)),
        out_shape=jax.ShapeDtypeStruct(qa.shape, _BF),
        compiler_params=_params(),
        name="win_attn",
    )(sink, qa, ka, ka, ka, vta, vta, vta, bias)


def _win_bias(n_heads):
    slopes = 2.0 ** (-8.0 * jnp.arange(1, n_heads + 1, dtype=_F32) / n_heads)
    k_pos = jnp.arange(3 * A_WINDOW)[:, None] - A_WINDOW
    q_pos = jnp.arange(A_WINDOW)[None, :]
    dist = jnp.abs(q_pos - k_pos)
    bias = -(LOG2E * slopes)[:, None, None] * dist.astype(_F32)[None]
    return jnp.where((dist <= A_WINDOW)[None], bias, NEG)


def _nbr_attn_kernel(q_ref, kwin_ref, vwin_ref, bias_ref, o_ref, *, rows, tiles_per_batch):
    rt = q_ref.shape[0] // GRID_W
    win_rows = kwin_ref.shape[0] // GRID_W
    n_heads = 2 * (q_ref.shape[1] // LANES)
    n_keys = NBR_KEY_ROWS * GRID_W
    n_chunks = n_keys // CHUNK
    first_row = (pl.program_id(0) % tiles_per_batch) * rt
    win_row0 = jnp.clip(first_row - B_KH // 2, 0, rows - win_rows)
    pending = {}

    def key_row0(jp):
        r = first_row + 2 * jp
        return jnp.clip(r - B_KH // 2, 0, rows - NBR_KEY_ROWS) - win_row0

    def score_job(n):
        jp, h = divmod(n, n_heads)
        r = first_row + 2 * jp
        k_off = pl.multiple_of(key_row0(jp) * GRID_W, CHUNK)
        combo = jnp.where(r == 0, 1, jnp.where(r == 2, 2, jnp.where(r == rows - 4, 3,
                                                                    jnp.where(r == rows - 2, 4, 0))))
        cols = slice((h // 2) * LANES, (h // 2 + 1) * LANES)
        q_tile = q_ref[jp * 2 * GRID_W:(jp + 1) * 2 * GRID_W, cols]
        return _scores(kwin_ref[pl.ds(k_off, n_keys), cols], _one_head(q_tile, h % 2),
                       lambda c: bias_ref[h, combo, c * CHUNK:(c + 1) * CHUNK], None)

    def value_job(n, m, p):
        jp, h = divmod(n, n_heads)
        c_off = key_row0(jp) // 2
        v_t = jnp.concatenate([vwin_ref[c_off + c, h * HEAD_DIM:(h + 1) * HEAD_DIM] for c in range(n_chunks)],
                              axis=1)
        pending[h % 2] = _values(_with_ones(v_t), None, m, p)
        if h % 2 == 1:
            o_ref[jp * 2 * GRID_W:(jp + 1) * 2 * GRID_W, (h // 2) * LANES:(h // 2 + 1) * LANES] = (
                jnp.concatenate([pending[0], pending[1]], axis=0).T.astype(_BF))

    _pipelined((rt // 2) * n_heads, score_job, value_job)


def _nbr_attn(qb, kb, vtb, bias, seq):
    n, width = qb.shape
    rows = seq // GRID_W
    rt = NBR_ROWS
    win_rows = rt + B_KH
    assert rows >= win_rows and rows % rt == 0 and win_rows % 2 == 0
    tile = rt * GRID_W
    tiles_per_batch = seq // tile
    kernel = functools.partial(_nbr_attn_kernel, rows=rows, tiles_per_batch=tiles_per_batch)

    def win_token(t):
        first_row = (t % tiles_per_batch) * rt
        win_row0 = jnp.clip(first_row - B_KH // 2, 0, rows - win_rows)
        return (t // tiles_per_batch) * seq + win_row0 * GRID_W

    return pl.pallas_call(
        kernel,
        grid=(n // tile,),
        in_specs=[pl.BlockSpec((tile, width), lambda t: (t, 0)),
                  pl.BlockSpec((pl.Element(win_rows * GRID_W), pl.Element(width)),
                               lambda t: (pl.multiple_of(win_token(t), CHUNK), 0)),
                  pl.BlockSpec((pl.Element(win_rows * GRID_W // CHUNK), pl.Element(width), pl.Element(CHUNK)),
                               lambda t: (win_token(t) // CHUNK, 0, 0)),
                  _resident(bias.shape)],
        out_specs=pl.BlockSpec((tile, width), lambda t: (t, 0)),
        out_shape=jax.ShapeDtypeStruct(qb.shape, _BF),
        compiler_params=_params(),
        name="nbr_attn",
    )(qb, kb, vtb, bias)


def _nbr_bias(rpb):
    variants = [(-4, 0, -5, 1), (0, 0, -1, 0), (-2, 0, -3, 0), (-6, 2, -7, 2), (-8, 2, -9, 2)]
    n_heads, n_dr, n_dc = rpb.shape
    n_kk = NBR_KEY_ROWS
    assert all(a2 == a1 - 1 for a1, _, a2, _ in variants)
    kc, c = np.meshgrid(np.arange(GRID_W), np.arange(GRID_W), indexing="ij")
    col_start = np.clip(c - B_KW // 2, 0, GRID_W - B_KW)
    col_ok = (kc >= col_start) & (kc < col_start + B_KW)
    spread = np.zeros((2, n_dc, GRID_W, 2, GRID_W), np.float32)
    for rs in range(2):
        for j in range(n_dc):
            spread[rs, j, :, rs, :] = (kc - c + (B_KW - 1)) == j
    spread = spread.reshape(2 * n_dc, GRID_W, 2 * GRID_W)
    n_blocks = max(a1 for a1, *_ in variants) + B_KH + n_kk
    rpb = rpb.astype(_F32)
    lhs = jnp.concatenate([jnp.pad(rpb, ((0, 0), (1 + rs, n_blocks - n_dr - 1 - rs), (0, 0))) for rs in range(2)],
                          axis=-1)
    blocks = jnp.einsum("hix,xkl->hikl", lhs, spread, precision=lax.Precision.HIGHEST)
    slabs = []
    for a1, b1, _, b2 in variants:
        kk = np.arange(n_kk)[:, None, None]
        row_ok = np.concatenate([np.broadcast_to((kk >= b) & (kk < b + B_KH), (n_kk, GRID_W, GRID_W))
                                 for b in (b1, b2)], axis=-1)
        ok = row_ok & np.tile(col_ok, (1, 2))[None]
        slabs.append(jnp.where(ok[None], blocks[:, a1 + B_KH:a1 + B_KH + n_kk], NEG))
    return jnp.stack(slabs, axis=1).reshape(n_heads, len(variants), n_kk * GRID_W, 2 * GRID_W)


def _out_mlp_kernel(x_ref, ya_ref, yb_ref, gate_ref, wpa_ref, wpb_ref, wo_ref, g2_ref, wup_ref, wdn_ref,
                    gf_ref, o_ref, *, final_norm):
    d = x_ref.shape[1]
    pa = jnp.dot(ya_ref[...], wpa_ref[...], preferred_element_type=_F32)
    pb = jnp.dot(yb_ref[...], wpb_ref[...], preferred_element_type=_F32)
    merged = gate_ref[:, :d].astype(_F32) * pa + gate_ref[:, d:].astype(_F32) * pb
    x1 = x_ref[...] + jnp.dot(merged.astype(_BF), wo_ref[...], preferred_element_type=_F32)
    h2 = _rms(x1, g2_ref[...]).astype(_BF)
    acc = x1
    d_ff = wup_ref.shape[1]
    for c in range(0, d_ff, FF_CHUNK):
        u = jnp.dot(h2, wup_ref[:, c:c + FF_CHUNK], preferred_element_type=_F32)
        a = jnp.square(jnp.maximum(u, 0.0)).astype(_BF)
        acc = acc + jnp.dot(a, wdn_ref[c:c + FF_CHUNK, :], preferred_element_type=_F32)
    o_ref[...] = _rms(acc, gf_ref[...]) if final_norm else acc


def _out_mlp(xf, ya, yb, gates, wpa, wpb, wo, g2, wup, wdn, gf, final_norm):
    n, d = xf.shape
    tm = min(TOKEN_TILE, n)
    row = lambda width: pl.BlockSpec((tm, width), lambda t: (t, 0))
    return pl.pallas_call(
        functools.partial(_out_mlp_kernel, final_norm=final_norm),
        grid=(n // tm,),
        in_specs=[row(d), row(ya.shape[1]), row(yb.shape[1]), row(gates.shape[1]),
                  _resident(wpa.shape), _resident(wpb.shape), _resident(wo.shape), _resident((1, d)),
                  _resident(wup.shape), _resident(wdn.shape), _resident((1, d))],
        out_specs=row(d),
        out_shape=jax.ShapeDtypeStruct((n, d), _F32),
        compiler_params=_params(),
        name="out_mlp",
    )(xf, ya, yb, gates, wpa, wpb, wo, g2, wup, wdn, gf)


def kernel(x, norm_mix, w_in, b_gate, sink, rpb, w_proj_a, w_proj_b, w_out, norm_mlp, w_up, w_down, norm_final):
    b, s, d = x.shape
    depth = w_in.shape[0]
    a_heads = sink.shape[1]
    b_heads = rpb.shape[1]
    a_w, kv_w, b_w = a_heads * HEAD_DIM, A_KV_HEADS * HEAD_DIM, b_heads * HEAD_DIM
    assert kv_w == LANES
    scale = LOG2E * HEAD_DIM ** -0.5
    win_bias = _win_bias(a_heads)
    xf = x.reshape(b * s, d)
    for l in range(depth):
        w = w_in[l]
        ka0, va0, qb0 = a_w, a_w + kv_w, a_w + 2 * kv_w
        kb0, vb0, g0 = qb0 + b_w, qb0 + 2 * b_w, qb0 + 3 * b_w
        col_scale = np.ones((1, w.shape[1]), np.float32)
        col_scale[:, :a_w] = scale
        col_scale[:, qb0:kb0] = scale
        col_of = {"qa": 0, "ka": ka0, "va": va0, "qb": qb0, "kb": kb0, "vb": vb0, "gate": g0}
        qa, ka, qb, kb, gates, vta, vtb = _in_proj(
            xf, norm_mix[l][None], (w * col_scale).astype(_BF), b_gate[l][None], col_of,
            (a_w, 2 * kv_w, b_w, b_w, 2 * d), (kv_w, b_w))
        ya = _win_attn(LOG2E * sink[l].astype(_F32), qa, ka, vta, win_bias, s)
        yb = _nbr_attn(qb, kb, vtb, _nbr_bias(LOG2E * rpb[l].astype(_F32)), s)
        xf = _out_mlp(xf, ya, yb, gates, w_proj_a[l].astype(_BF), w_proj_b[l].astype(_BF), w_out[l].astype(_BF),
                      norm_mlp[l][None], w_up[l].astype(_BF), w_down[l].astype(_BF), norm_final[None],
                      final_norm=(l == depth - 1))
    return xf.reshape(b, s, d)
```

```python
import functools

import numpy as np
import jax
import jax.numpy as jnp
from jax import lax
from jax.experimental import pallas as pl
from jax.experimental.pallas import tpu as pltpu

HEAD_DIM = 64
LANES = 128
CHUNK = 128
A_KV_HEADS = 2
A_WINDOW = 128
GRID_W = 64
B_KH = 8
B_KW = 16
NBR_KEY_ROWS = 10
EPS = 1e-6
NEG = -1e30
LOG2E = 1.4426950408889634
ONES_ROWS = 16

IN_TILE = 1024
TOKEN_TILE = 512
WIN_TILE = 2048
NBR_ROWS = 32
FF_CHUNK = 512
VMEM_LIMIT = 56 * 1024 * 1024

_BF = jnp.bfloat16
_F32 = jnp.float32
_NT = (((1,), (1,)), ((), ()))


def _rms(x, g):
    var = jnp.mean(x * x, axis=-1, keepdims=True)
    return (x * lax.rsqrt(var + EPS)) * g


def _resident(shape):
    zeros = (0,) * len(shape)
    return pl.BlockSpec(shape, lambda t: zeros, pipeline_mode=pl.Buffered(1))


def _params():
    return pltpu.CompilerParams(dimension_semantics=("arbitrary",), vmem_limit_bytes=VMEM_LIMIT)


def _in_proj_kernel(x_ref, g_ref, w_ref, wvt_ref, bg_ref, qa_ref, ka_ref, qb_ref, kb_ref, gate_ref, vta_ref,
                    vtb_ref, *, col_of):
    hb = _rms(x_ref[...], g_ref[...]).astype(_BF)

    def proj(name, width):
        off = col_of[name]
        return jnp.dot(hb, w_ref[:, off:off + width], preferred_element_type=_F32)

    qa_ref[...] = proj("qa", qa_ref.shape[1]).astype(_BF)
    qb_ref[...] = proj("qb", qb_ref.shape[1]).astype(_BF)
    kb_ref[...] = proj("kb", kb_ref.shape[1]).astype(_BF)
    k = proj("ka", LANES)
    k_swapped = pltpu.roll(k, HEAD_DIM, axis=1)
    lo = lax.broadcasted_iota(jnp.int32, k.shape, 1) < HEAD_DIM
    ka_ref[:, :LANES] = jnp.where(lo, k, k_swapped).astype(_BF)
    ka_ref[:, LANES:] = jnp.where(lo, k_swapped, k).astype(_BF)
    gate_w = gate_ref.shape[1]
    step = 512
    for c in range(0, gate_w, step):
        g = jnp.dot(hb, w_ref[:, col_of["gate"] + c:col_of["gate"] + c + step], preferred_element_type=_F32)
        gate_ref[:, c:c + step] = jax.nn.sigmoid(g + bg_ref[:, c:c + step]).astype(_BF)
    vt = lax.dot_general(wvt_ref[...], hb, _NT, preferred_element_type=_F32).astype(_BF)
    va = vta_ref.shape[1]
    for c in range(vta_ref.shape[0]):
        vta_ref[c] = vt[:va, c * CHUNK:(c + 1) * CHUNK]
        vtb_ref[c] = vt[va:, c * CHUNK:(c + 1) * CHUNK]


def _in_proj(xf, g, w, wvt, bg, col_of, widths, v_widths):
    n, d = xf.shape
    tm = min(IN_TILE, n)
    out_shape = [jax.ShapeDtypeStruct((n, wd), _BF) for wd in widths]
    out_shape += [jax.ShapeDtypeStruct((n // CHUNK, vw, CHUNK), _BF) for vw in v_widths]
    out_specs = [pl.BlockSpec((tm, wd), lambda t: (t, 0)) for wd in widths]
    out_specs += [pl.BlockSpec((tm // CHUNK, vw, CHUNK), lambda t: (t, 0, 0)) for vw in v_widths]
    return pl.pallas_call(
        functools.partial(_in_proj_kernel, col_of=col_of),
        grid=(n // tm,),
        in_specs=[pl.BlockSpec((tm, d), lambda t: (t, 0)),
                  _resident((1, d)), _resident(w.shape), _resident(wvt.shape), _resident(bg.shape)],
        out_specs=out_specs,
        out_shape=out_shape,
        compiler_params=_params(),
        name="in_proj",
    )(xf, g, w, wvt, bg)


def _transpose_kernel(starts_ref, w_ref, o_ref):
    o_ref[...] = w_ref[...].T.astype(_BF)


def _transposed_columns(w, col_starts):
    d = w.shape[0]
    assert all(c % LANES == 0 for c in col_starts)
    blocks = jnp.asarray([c // LANES for c in col_starts], jnp.int32)
    return pl.pallas_call(
        _transpose_kernel,
        grid_spec=pltpu.PrefetchScalarGridSpec(
            num_scalar_prefetch=1, grid=(len(col_starts),),
            in_specs=[pl.BlockSpec((d, LANES), lambda t, starts: (0, starts[t]))],
            out_specs=pl.BlockSpec((LANES, d), lambda t, starts: (t, 0))),
        out_shape=jax.ShapeDtypeStruct((LANES * len(col_starts), d), _BF),
        compiler_params=_params(),
        name="transpose_value_weights",
    )(blocks, w)


def _with_ones(vt):
    return jnp.concatenate([vt, jnp.ones((ONES_ROWS, vt.shape[1]), vt.dtype)], axis=0)


def _scores(k_all, qm, bias_chunk, sink):
    s_all = lax.dot_general(k_all, qm, _NT, preferred_element_type=_F32)
    chunks = [s_all[c * CHUNK:(c + 1) * CHUNK] + bias_chunk(c) for c in range(k_all.shape[0] // CHUNK)]
    m = jnp.max(functools.reduce(jnp.maximum, chunks), axis=0, keepdims=True)
    return chunks, (m if sink is None else jnp.maximum(m, sink))


def _probs(chunks, m):
    return jnp.concatenate([jnp.exp2(s_c - m).astype(_BF) for s_c in chunks], axis=0)


def _values(vt_aug, sink, m, p):
    oa = jnp.dot(vt_aug, p, preferred_element_type=_F32)
    l = oa[HEAD_DIM:HEAD_DIM + 1]
    if sink is not None:
        l = l + jnp.exp2(sink - m)
    return oa[:HEAD_DIM] * (1.0 / l)


def _pipelined(n_jobs, score_job, value_job):
    scored, probs = {}, {}
    for step in range(n_jobs + 2):
        if step < n_jobs:
            scored[step] = score_job(step)
        if 0 <= step - 2 < n_jobs:
            value_job(step - 2, *probs.pop(step - 2))
        if 0 <= step - 1 < n_jobs:
            chunks, m = scored.pop(step - 1)
            probs[step - 1] = (m, _probs(chunks, m))


def _one_head(q_tile, half):
    lane = lax.broadcasted_iota(jnp.int32, q_tile.shape, 1)
    keep = (lane < HEAD_DIM) if half == 0 else (lane >= HEAD_DIM)
    return jnp.where(keep, q_tile, jnp.zeros_like(q_tile))


def _win_attn_kernel(sink_ref, q_ref, kp_ref, kc_ref, kn_ref, vp_ref, vc_ref, vn_ref, bias_ref, o_ref,
                     *, tiles_per_batch):
    blk = A_WINDOW
    nsub = q_ref.shape[0] // blk
    n_pairs = q_ref.shape[1] // LANES
    i = pl.program_id(0) % tiles_per_batch
    first_neg = jnp.where(i == 0, NEG, 0.0)
    last_neg = jnp.where(i == tiles_per_batch - 1, NEG, 0.0)
    n_heads = 2 * n_pairs
    grp = n_heads // A_KV_HEADS
    pending = {}

    def key_block(b, kv):
        cols = slice(kv * LANES, (kv + 1) * LANES)
        if b < 0:
            return kp_ref[:, cols]
        return kn_ref[:, cols] if b == nsub else kc_ref[b * blk:(b + 1) * blk, cols]

    def score_job(n):
        j, h = divmod(n, n_heads)

        def bias_chunk(c):
            b = bias_ref[h, c * blk:(c + 1) * blk]
            if j == 0 and c == 0:
                b = b + first_neg
            if j == nsub - 1 and c == 2:
                b = b + last_neg
            return b

        q_tile = q_ref[j * blk:(j + 1) * blk, (h // 2) * LANES:(h // 2 + 1) * LANES]
        k_all = jnp.concatenate([key_block(j - 1 + c, h // grp) for c in range(3)], axis=0)
        return _scores(k_all, _one_head(q_tile, h % 2), bias_chunk, sink_ref[h])

    def value_job(n, m, p):
        j, h = divmod(n, n_heads)
        kv = slice((h // grp) * HEAD_DIM, (h // grp + 1) * HEAD_DIM)
        v_t = jnp.concatenate(
            [vp_ref[0, kv] if j == 0 else vc_ref[j - 1, kv], vc_ref[j, kv],
             vn_ref[0, kv] if j == nsub - 1 else vc_ref[j + 1, kv]], axis=1)
        pending[h % 2] = _values(_with_ones(v_t), sink_ref[h], m, p)
        if h % 2 == 1:
            o_ref[j * blk:(j + 1) * blk, (h // 2) * LANES:(h // 2 + 1) * LANES] = (
                jnp.concatenate([pending[0], pending[1]], axis=0).T.astype(_BF))

    _pipelined(nsub * n_heads, score_job, value_job)


def _win_attn(sink, qa, ka, vta, bias, seq):
    n = qa.shape[0]
    tq = min(WIN_TILE, seq)
    blk = A_WINDOW
    sub = tq // blk
    n_blk = n // blk
    prev_blk = lambda t: jnp.maximum(t * sub - 1, 0)
    next_blk = lambda t: jnp.minimum((t + 1) * sub, n_blk - 1)
    kernel = functools.partial(_win_attn_kernel, tiles_per_batch=seq // tq)
    kw, vw = ka.shape[1], vta.shape[1]
    return pl.pallas_call(
        kernel,
        grid=(n // tq,),
        in_specs=[pl.BlockSpec(memory_space=pltpu.SMEM),
                  pl.BlockSpec((tq, qa.shape[1]), lambda t: (t, 0)),
                  pl.BlockSpec((blk, kw), lambda t: (prev_blk(t), 0)),
                  pl.BlockSpec((tq, kw), lambda t: (t, 0)),
                  pl.BlockSpec((blk, kw), lambda t: (next_blk(t), 0)),
                  pl.BlockSpec((1, vw, CHUNK), lambda t: (prev_blk(t), 0, 0)),
                  pl.BlockSpec((sub, vw, CHUNK), lambda t: (t, 0, 0)),
                  pl.BlockSpec((1, vw, CHUNK), lambda t: (next_blk(t), 0, 0)),
                  _resident(bias.shape)],
        out_specs=pl.BlockSpec((tq, qa.shape[1]), lambda t: (t, 0)),
        out_shape=jax.ShapeDtypeStruct(qa.shape, _BF),
        compiler_params=_params(),
        name="win_attn",
    )(sink, qa, ka, ka, ka, vta, vta, vta, bias)


def _win_bias(n_heads):
    slopes = 2.0 ** (-8.0 * jnp.arange(1, n_heads + 1, dtype=_F32) / n_heads)
    k_pos = jnp.arange(3 * A_WINDOW)[:, None] - A_WINDOW
    q_pos = jnp.arange(A_WINDOW)[None, :]
    dist = jnp.abs(q_pos - k_pos)
    bias = -(LOG2E * slopes)[:, None, None] * dist.astype(_F32)[None]
    return jnp.where((dist <= A_WINDOW)[None], bias, NEG)


def _nbr_attn_kernel(q_ref, kwin_ref, vwin_ref, bias_ref, o_ref, *, rows, tiles_per_batch):
    rt = q_ref.shape[0] // GRID_W
    win_rows = kwin_ref.shape[0] // GRID_W
    n_heads = 2 * (q_ref.shape[1] // LANES)
    n_keys = NBR_KEY_ROWS * GRID_W
    n_chunks = n_keys // CHUNK
    first_row = (pl.program_id(0) % tiles_per_batch) * rt
    win_row0 = jnp.clip(first_row - B_KH // 2, 0, rows - win_rows)
    pending = {}

    def key_row0(jp):
        r = first_row + 2 * jp
        return jnp.clip(r - B_KH // 2, 0, rows - NBR_KEY_ROWS) - win_row0

    def score_job(n):
        jp, h = divmod(n, n_heads)
        r = first_row + 2 * jp
        k_off = pl.multiple_of(key_row0(jp) * GRID_W, CHUNK)
        combo = jnp.where(r == 0, 1, jnp.where(r == 2, 2, jnp.where(r == rows - 4, 3,
                                                                    jnp.where(r == rows - 2, 4, 0))))
        cols = slice((h // 2) * LANES, (h // 2 + 1) * LANES)
        q_tile = q_ref[jp * 2 * GRID_W:(jp + 1) * 2 * GRID_W, cols]
        return _scores(kwin_ref[pl.ds(k_off, n_keys), cols], _one_head(q_tile, h % 2),
                       lambda c: bias_ref[h, combo, c * CHUNK:(c + 1) * CHUNK], None)

    def value_job(n, m, p):
        jp, h = divmod(n, n_heads)
        c_off = key_row0(jp) // 2
        v_t = jnp.concatenate([vwin_ref[c_off + c, h * HEAD_DIM:(h + 1) * HEAD_DIM] for c in range(n_chunks)],
                              axis=1)
        pending[h % 2] = _values(_with_ones(v_t), None, m, p)
        if h % 2 == 1:
            o_ref[jp * 2 * GRID_W:(jp + 1) * 2 * GRID_W, (h // 2) * LANES:(h // 2 + 1) * LANES] = (
                jnp.concatenate([pending[0], pending[1]], axis=0).T.astype(_BF))

    _pipelined((rt // 2) * n_heads, score_job, value_job)


def _nbr_attn(qb, kb, vtb, bias, seq):
    n, width = qb.shape
    rows = seq // GRID_W
    rt = NBR_ROWS
    win_rows = rt + B_KH
    assert rows >= win_rows and rows % rt == 0 and win_rows % 2 == 0
    tile = rt * GRID_W
    tiles_per_batch = seq // tile
    kernel = functools.partial(_nbr_attn_kernel, rows=rows, tiles_per_batch=tiles_per_batch)

    def win_token(t):
        first_row = (t % tiles_per_batch) * rt
        win_row0 = jnp.clip(first_row - B_KH // 2, 0, rows - win_rows)
        return (t // tiles_per_batch) * seq + win_row0 * GRID_W

    return pl.pallas_call(
        kernel,
        grid=(n // tile,),
        in_specs=[pl.BlockSpec((tile, width), lambda t: (t, 0)),
                  pl.BlockSpec((pl.Element(win_rows * GRID_W), pl.Element(width)),
                               lambda t: (pl.multiple_of(win_token(t), CHUNK), 0)),
                  pl.BlockSpec((pl.Element(win_rows * GRID_W // CHUNK), pl.Element(width), pl.Element(CHUNK)),
                               lambda t: (win_token(t) // CHUNK, 0, 0)),
                  _resident(bias.shape)],
        out_specs=pl.BlockSpec((tile, width), lambda t: (t, 0)),
        out_shape=jax.ShapeDtypeStruct(qb.shape, _BF),
        compiler_params=_params(),
        name="nbr_attn",
    )(qb, kb, vtb, bias)


def _nbr_bias(rpb):
    variants = [(-4, 0, -5, 1), (0, 0, -1, 0), (-2, 0, -3, 0), (-6, 2, -7, 2), (-8, 2, -9, 2)]
    n_heads, n_dr, n_dc = rpb.shape
    n_kk = NBR_KEY_ROWS
    assert all(a2 == a1 - 1 for a1, _, a2, _ in variants)
    kc, c = np.meshgrid(np.arange(GRID_W), np.arange(GRID_W), indexing="ij")
    col_start = np.clip(c - B_KW // 2, 0, GRID_W - B_KW)
    col_ok = (kc >= col_start) & (kc < col_start + B_KW)
    spread = np.zeros((2, n_dc, GRID_W, 2, GRID_W), np.float32)
    for rs in range(2):
        for j in range(n_dc):
            spread[rs, j, :, rs, :] = (kc - c + (B_KW - 1)) == j
    spread = spread.reshape(2 * n_dc, GRID_W, 2 * GRID_W)
    n_blocks = max(a1 for a1, *_ in variants) + B_KH + n_kk
    rpb = rpb.astype(_F32)
    lhs = jnp.concatenate([jnp.pad(rpb, ((0, 0), (1 + rs, n_blocks - n_dr - 1 - rs), (0, 0))) for rs in range(2)],
                          axis=-1)
    blocks = jnp.einsum("hix,xkl->hikl", lhs, spread, precision=lax.Precision.HIGHEST)
    slabs = []
    for a1, b1, _, b2 in variants:
        kk = np.arange(n_kk)[:, None, None]
        row_ok = np.concatenate([np.broadcast_to((kk >= b) & (kk < b + B_KH), (n_kk, GRID_W, GRID_W))
                                 for b in (b1, b2)], axis=-1)
        ok = row_ok & np.tile(col_ok, (1, 2))[None]
        slabs.append(jnp.where(ok[None], blocks[:, a1 + B_KH:a1 + B_KH + n_kk], NEG))
    return jnp.stack(slabs, axis=1).reshape(n_heads, len(variants), n_kk * GRID_W, 2 * GRID_W)


def _out_mlp_kernel(x_ref, ya_ref, yb_ref, gate_ref, wpa_ref, wpb_ref, wo_ref, g2_ref, wup_ref, wdn_ref,
                    gf_ref, o_ref, *, final_norm):
    d = x_ref.shape[1]
    pa = jnp.dot(ya_ref[...], wpa_ref[...], preferred_element_type=_F32)
    pb = jnp.dot(yb_ref[...], wpb_ref[...], preferred_element_type=_F32)
    merged = gate_ref[:, :d].astype(_F32) * pa + gate_ref[:, d:].astype(_F32) * pb
    x1 = x_ref[...] + jnp.dot(merged.astype(_BF), wo_ref[...], preferred_element_type=_F32)
    h2 = _rms(x1, g2_ref[...]).astype(_BF)
    acc = x1
    d_ff = wup_ref.shape[1]
    for c in range(0, d_ff, FF_CHUNK):
        u = jnp.dot(h2, wup_ref[:, c:c + FF_CHUNK], preferred_element_type=_F32)
        a = jnp.square(jnp.maximum(u, 0.0)).astype(_BF)
        acc = acc + jnp.dot(a, wdn_ref[c:c + FF_CHUNK, :], preferred_element_type=_F32)
    o_ref[...] = _rms(acc, gf_ref[...]) if final_norm else acc


def _out_mlp(xf, ya, yb, gates, wpa, wpb, wo, g2, wup, wdn, gf, final_norm):
    n, d = xf.shape
    tm = min(TOKEN_TILE, n)
    row = lambda width: pl.BlockSpec((tm, width), lambda t: (t, 0))
    return pl.pallas_call(
        functools.partial(_out_mlp_kernel, final_norm=final_norm),
        grid=(n // tm,),
        in_specs=[row(d), row(ya.shape[1]), row(yb.shape[1]), row(gates.shape[1]),
                  _resident(wpa.shape), _resident(wpb.shape), _resident(wo.shape), _resident((1, d)),
                  _resident(wup.shape), _resident(wdn.shape), _resident((1, d))],
        out_specs=row(d),
        out_shape=jax.ShapeDtypeStruct((n, d), _F32),
        compiler_params=_params(),
        name="out_mlp",
    )(xf, ya, yb, gates, wpa, wpb, wo, g2, wup, wdn, gf)


def kernel(x, norm_mix, w_in, b_gate, sink, rpb, w_proj_a, w_proj_b, w_out, norm_mlp, w_up, w_down, norm_final):
    b, s, d = x.shape
    depth = w_in.shape[0]
    a_heads = sink.shape[1]
    b_heads = rpb.shape[1]
    a_w, kv_w, b_w = a_heads * HEAD_DIM, A_KV_HEADS * HEAD_DIM, b_heads * HEAD_DIM
    assert kv_w == LANES
    scale = LOG2E * HEAD_DIM ** -0.5
    win_bias = _win_bias(a_heads)
    xf = x.reshape(b * s, d)
    for l in range(depth):
        w = w_in[l]
        ka0, va0, qb0 = a_w, a_w + kv_w, a_w + 2 * kv_w
        kb0, vb0, g0 = qb0 + b_w, qb0 + 2 * b_w, qb0 + 3 * b_w
        col_scale = np.ones((1, w.shape[1]), np.float32)
        col_scale[:, :a_w] = scale
        col_scale[:, qb0:kb0] = scale
        col_of = {"qa": 0, "ka": ka0, "va": va0, "qb": qb0, "kb": kb0, "vb": vb0, "gate": g0}
        w_vt = _transposed_columns(w, [va0 + c for c in range(0, kv_w, LANES)]
                                   + [vb0 + c for c in range(0, b_w, LANES)])
        qa, ka, qb, kb, gates, vta, vtb = _in_proj(
            xf, norm_mix[l][None], (w * col_scale).astype(_BF), w_vt, b_gate[l][None], col_of,
            (a_w, 2 * kv_w, b_w, b_w, 2 * d), (kv_w, b_w))
        ya = _win_attn(LOG2E * sink[l].astype(_F32), qa, ka, vta, win_bias, s)
        yb = _nbr_attn(qb, kb, vtb, _nbr_bias(LOG2E * rpb[l].astype(_F32)), s)
        xf = _out_mlp(xf, ya, yb, gates, w_proj_a[l].astype(_BF), w_proj_b[l].astype(_BF), w_out[l].astype(_BF),
                      norm_mlp[l][None], w_up[l].astype(_BF), w_down[l].astype(_BF), norm_final[None],
                      final_norm=(l == depth - 1))
    return xf.reshape(b, s, d)
```

```python
import functools

import numpy as np
import jax
import jax.numpy as jnp
from jax import lax
from jax.experimental import pallas as pl
from jax.experimental.pallas import tpu as pltpu

HEAD_DIM = 64
LANES = 128
CHUNK = 128
A_KV_HEADS = 2
A_WINDOW = 128
GRID_W = 64
B_KH = 8
B_KW = 16
NBR_KEY_ROWS = 10
EPS = 1e-6
NEG = -1e30
LOG2E = 1.4426950408889634
ONES_ROWS = 16
PROBS_LAG = 3
VALUES_LAG = 7

IN_TILE = 1024
TOKEN_TILE = 512
WIN_TILE = 2048
NBR_ROWS = 32
FF_CHUNK = 512
VMEM_LIMIT = 56 * 1024 * 1024

_BF = jnp.bfloat16
_F32 = jnp.float32
_NT = (((1,), (1,)), ((), ()))


def _rms(x, g):
    var = jnp.mean(x * x, axis=-1, keepdims=True)
    return (x * lax.rsqrt(var + EPS)) * g


def _resident(shape):
    zeros = (0,) * len(shape)
    return pl.BlockSpec(shape, lambda t: zeros, pipeline_mode=pl.Buffered(1))


def _params():
    return pltpu.CompilerParams(dimension_semantics=("arbitrary",), vmem_limit_bytes=VMEM_LIMIT)


def _in_proj_kernel(x_ref, g_ref, w_ref, wvt_ref, bg_ref, qa_ref, ka_ref, qb_ref, kb_ref, gate_ref, vta_ref,
                    vtb_ref, *, col_of):
    hb = _rms(x_ref[...], g_ref[...]).astype(_BF)

    def proj(name, width):
        off = col_of[name]
        return jnp.dot(hb, w_ref[:, off:off + width], preferred_element_type=_F32)

    qa_ref[...] = proj("qa", qa_ref.shape[1]).astype(_BF)
    qb_ref[...] = proj("qb", qb_ref.shape[1]).astype(_BF)
    kb_ref[...] = proj("kb", kb_ref.shape[1]).astype(_BF)
    k = proj("ka", LANES)
    k_swapped = pltpu.roll(k, HEAD_DIM, axis=1)
    lo = lax.broadcasted_iota(jnp.int32, k.shape, 1) < HEAD_DIM
    ka_ref[:, :LANES] = jnp.where(lo, k, k_swapped).astype(_BF)
    ka_ref[:, LANES:] = jnp.where(lo, k_swapped, k).astype(_BF)
    gate_w = gate_ref.shape[1]
    step = 512
    for c in range(0, gate_w, step):
        g = jnp.dot(hb, w_ref[:, col_of["gate"] + c:col_of["gate"] + c + step], preferred_element_type=_F32)
        gate_ref[:, c:c + step] = jax.nn.sigmoid(g + bg_ref[:, c:c + step]).astype(_BF)
    vt = lax.dot_general(wvt_ref[...], hb, _NT, preferred_element_type=_F32).astype(_BF)
    va = vta_ref.shape[1]
    for c in range(vta_ref.shape[0]):
        vta_ref[c] = vt[:va, c * CHUNK:(c + 1) * CHUNK]
        vtb_ref[c] = vt[va:, c * CHUNK:(c + 1) * CHUNK]


def _in_proj(xf, g, w, wvt, bg, col_of, widths, v_widths):
    n, d = xf.shape
    tm = min(IN_TILE, n)
    out_shape = [jax.ShapeDtypeStruct((n, wd), _BF) for wd in widths]
    out_shape += [jax.ShapeDtypeStruct((n // CHUNK, vw, CHUNK), _BF) for vw in v_widths]
    out_specs = [pl.BlockSpec((tm, wd), lambda t: (t, 0)) for wd in widths]
    out_specs += [pl.BlockSpec((tm // CHUNK, vw, CHUNK), lambda t: (t, 0, 0)) for vw in v_widths]
    return pl.pallas_call(
        functools.partial(_in_proj_kernel, col_of=col_of),
        grid=(n // tm,),
        in_specs=[pl.BlockSpec((tm, d), lambda t: (t, 0)),
                  _resident((1, d)), _resident(w.shape), _resident(wvt.shape), _resident(bg.shape)],
        out_specs=out_specs,
        out_shape=out_shape,
        compiler_params=_params(),
        name="in_proj",
    )(xf, g, w, wvt, bg)


def _transpose_kernel(starts_ref, w_ref, o_ref):
    o_ref[...] = w_ref[...].T.astype(_BF)


def _transposed_columns(w, col_starts):
    d = w.shape[0]
    assert all(c % LANES == 0 for c in col_starts)
    blocks = jnp.asarray([c // LANES for c in col_starts], jnp.int32)
    return pl.pallas_call(
        _transpose_kernel,
        grid_spec=pltpu.PrefetchScalarGridSpec(
            num_scalar_prefetch=1, grid=(len(col_starts),),
            in_specs=[pl.BlockSpec((d, LANES), lambda t, starts: (0, starts[t]))],
            out_specs=pl.BlockSpec((LANES, d), lambda t, starts: (t, 0))),
        out_shape=jax.ShapeDtypeStruct((LANES * len(col_starts), d), _BF),
        compiler_params=_params(),
        name="transpose_value_weights",
    )(blocks, w)


def _with_ones(vt):
    return jnp.concatenate([vt, jnp.ones((ONES_ROWS, vt.shape[1]), vt.dtype)], axis=0)


def _scores(k_all, qm, bias_chunk, sink):
    s_all = lax.dot_general(k_all, qm, _NT, preferred_element_type=_F32)
    chunks = [s_all[c * CHUNK:(c + 1) * CHUNK] + bias_chunk(c) for c in range(k_all.shape[0] // CHUNK)]
    m = jnp.max(functools.reduce(jnp.maximum, chunks), axis=0, keepdims=True)
    return chunks, (m if sink is None else jnp.maximum(m, sink))


def _probs(chunks, m):
    return jnp.concatenate([jnp.exp2(s_c - m).astype(_BF) for s_c in chunks], axis=0)


def _values(vt_aug, sink, m, p):
    oa = jnp.dot(vt_aug, p, preferred_element_type=_F32)
    l = oa[HEAD_DIM:HEAD_DIM + 1]
    if sink is not None:
        l = l + jnp.exp2(sink - m)
    return oa[:HEAD_DIM] * (1.0 / l)


def _pipelined(n_jobs, score_job, value_job):
    scored, probs = {}, {}
    for step in range(n_jobs + VALUES_LAG):
        if step < n_jobs:
            scored[step] = score_job(step)
        if 0 <= step - VALUES_LAG < n_jobs:
            value_job(step - VALUES_LAG, *probs.pop(step - VALUES_LAG))
        if 0 <= step - PROBS_LAG < n_jobs:
            chunks, m = scored.pop(step - PROBS_LAG)
            probs[step - PROBS_LAG] = (m, _probs(chunks, m))


def _one_head(q_tile, half):
    lane = lax.broadcasted_iota(jnp.int32, q_tile.shape, 1)
    keep = (lane < HEAD_DIM) if half == 0 else (lane >= HEAD_DIM)
    return jnp.where(keep, q_tile, jnp.zeros_like(q_tile))


def _win_attn_kernel(sink_ref, q_ref, kp_ref, kc_ref, kn_ref, vp_ref, vc_ref, vn_ref, bias_ref, o_ref,
                     *, tiles_per_batch):
    blk = A_WINDOW
    nsub = q_ref.shape[0] // blk
    n_pairs = q_ref.shape[1] // LANES
    i = pl.program_id(0) % tiles_per_batch
    first_neg = jnp.where(i == 0, NEG, 0.0)
    last_neg = jnp.where(i == tiles_per_batch - 1, NEG, 0.0)
    n_heads = 2 * n_pairs
    grp = n_heads // A_KV_HEADS
    pending = {}

    def key_block(b, kv):
        cols = slice(kv * LANES, (kv + 1) * LANES)
        if b < 0:
            return kp_ref[:, cols]
        return kn_ref[:, cols] if b == nsub else kc_ref[b * blk:(b + 1) * blk, cols]

    def score_job(n):
        j, h = divmod(n, n_heads)

        def bias_chunk(c):
            b = bias_ref[h, c * blk:(c + 1) * blk]
            if j == 0 and c == 0:
                b = b + first_neg
            if j == nsub - 1 and c == 2:
                b = b + last_neg
            return b

        q_tile = q_ref[j * blk:(j + 1) * blk, (h // 2) * LANES:(h // 2 + 1) * LANES]
        k_all = jnp.concatenate([key_block(j - 1 + c, h // grp) for c in range(3)], axis=0)
        return _scores(k_all, _one_head(q_tile, h % 2), bias_chunk, sink_ref[h])

    def value_job(n, m, p):
        j, h = divmod(n, n_heads)
        kv = slice((h // grp) * HEAD_DIM, (h // grp + 1) * HEAD_DIM)
        v_t = jnp.concatenate(
            [vp_ref[0, kv] if j == 0 else vc_ref[j - 1, kv], vc_ref[j, kv],
             vn_ref[0, kv] if j == nsub - 1 else vc_ref[j + 1, kv]], axis=1)
        pending[h % 2] = _values(_with_ones(v_t), sink_ref[h], m, p)
        if h % 2 == 1:
            o_ref[j * blk:(j + 1) * blk, (h // 2) * LANES:(h // 2 + 1) * LANES] = (
                jnp.concatenate([pending[0], pending[1]], axis=0).T.astype(_BF))

    _pipelined(nsub * n_heads, score_job, value_job)


def _win_attn(sink, qa, ka, vta, bias, seq):
    n = qa.shape[0]
    tq = min(WIN_TILE, seq)
    blk = A_WINDOW
    sub = tq // blk
    n_blk = n // blk
    prev_blk = lambda t: jnp.maximum(t * sub - 1, 0)
    next_blk = lambda t: jnp.minimum((t + 1) * sub, n_blk - 1)
    kernel = functools.partial(_win_attn_kernel, tiles_per_batch=seq // tq)
    kw, vw = ka.shape[1], vta.shape[1]
    return pl.pallas_call(
        kernel,
        grid=(n // tq,),
        in_specs=[pl.BlockSpec(memory_space=pltpu.SMEM),
                  pl.BlockSpec((tq, qa.shape[1]), lambda t: (t, 0)),
                  pl.BlockSpec((blk, kw), lambda t: (prev_blk(t), 0)),
                  pl.BlockSpec((tq, kw), lambda t: (t, 0)),
                  pl.BlockSpec((blk, kw), lambda t: (next_blk(t), 0)),
                  pl.BlockSpec((1, vw, CHUNK), lambda t: (prev_blk(t), 0, 0)),
                  pl.BlockSpec((sub, vw, CHUNK), lambda t: (t, 0, 0)),
                  pl.BlockSpec((1, vw, CHUNK), lambda t: (next_blk(t), 0, 0)),
                  _resident(bias.shape)],
        out_specs=pl.BlockSpec((tq, qa.shape[1]), lambda t: (t, 0)),
        out_shape=jax.ShapeDtypeStruct(qa.shape, _BF),
        compiler_params=_params(),
        name="win_attn",
    )(sink, qa, ka, ka, ka, vta, vta, vta, bias)


def _win_bias(n_heads):
    slopes = 2.0 ** (-8.0 * jnp.arange(1, n_heads + 1, dtype=_F32) / n_heads)
    k_pos = jnp.arange(3 * A_WINDOW)[:, None] - A_WINDOW
    q_pos = jnp.arange(A_WINDOW)[None, :]
    dist = jnp.abs(q_pos - k_pos)
    bias = -(LOG2E * slopes)[:, None, None] * dist.astype(_F32)[None]
    return jnp.where((dist <= A_WINDOW)[None], bias, NEG)


def _nbr_attn_kernel(q_ref, kwin_ref, vwin_ref, bias_ref, o_ref, *, rows, tiles_per_batch):
    rt = q_ref.shape[0] // GRID_W
    win_rows = kwin_ref.shape[0] // GRID_W
    n_heads = 2 * (q_ref.shape[1] // LANES)
    n_keys = NBR_KEY_ROWS * GRID_W
    n_chunks = n_keys // CHUNK
    first_row = (pl.program_id(0) % tiles_per_batch) * rt
    win_row0 = jnp.clip(first_row - B_KH // 2, 0, rows - win_rows)
    pending = {}

    def key_row0(jp):
        r = first_row + 2 * jp
        return jnp.clip(r - B_KH // 2, 0, rows - NBR_KEY_ROWS) - win_row0

    def score_job(n):
        jp, h = divmod(n, n_heads)
        r = first_row + 2 * jp
        k_off = pl.multiple_of(key_row0(jp) * GRID_W, CHUNK)
        combo = jnp.where(r == 0, 1, jnp.where(r == 2, 2, jnp.where(r == rows - 4, 3,
                                                                    jnp.where(r == rows - 2, 4, 0))))
        cols = slice((h // 2) * LANES, (h // 2 + 1) * LANES)
        q_tile = q_ref[jp * 2 * GRID_W:(jp + 1) * 2 * GRID_W, cols]
        return _scores(kwin_ref[pl.ds(k_off, n_keys), cols], _one_head(q_tile, h % 2),
                       lambda c: bias_ref[h, combo, c * CHUNK:(c + 1) * CHUNK], None)

    def value_job(n, m, p):
        jp, h = divmod(n, n_heads)
        c_off = key_row0(jp) // 2
        v_t = jnp.concatenate([vwin_ref[c_off + c, h * HEAD_DIM:(h + 1) * HEAD_DIM] for c in range(n_chunks)],
                              axis=1)
        pending[h % 2] = _values(_with_ones(v_t), None, m, p)
        if h % 2 == 1:
            o_ref[jp * 2 * GRID_W:(jp + 1) * 2 * GRID_W, (h // 2) * LANES:(h // 2 + 1) * LANES] = (
                jnp.concatenate([pending[0], pending[1]], axis=0).T.astype(_BF))

    _pipelined((rt // 2) * n_heads, score_job, value_job)


def _nbr_attn(qb, kb, vtb, bias, seq):
    n, width = qb.shape
    rows = seq // GRID_W
    rt = NBR_ROWS
    win_rows = rt + B_KH
    assert rows >= win_rows and rows % rt == 0 and win_rows % 2 == 0
    tile = rt * GRID_W
    tiles_per_batch = seq // tile
    kernel = functools.partial(_nbr_attn_kernel, rows=rows, tiles_per_batch=tiles_per_batch)

    def win_token(t):
        first_row = (t % tiles_per_batch) * rt
        win_row0 = jnp.clip(first_row - B_KH // 2, 0, rows - win_rows)
        return (t // tiles_per_batch) * seq + win_row0 * GRID_W

    return pl.pallas_call(
        kernel,
        grid=(n // tile,),
        in_specs=[pl.BlockSpec((tile, width), lambda t: (t, 0)),
                  pl.BlockSpec((pl.Element(win_rows * GRID_W), pl.Element(width)),
                               lambda t: (pl.multiple_of(win_token(t), CHUNK), 0)),
                  pl.BlockSpec((pl.Element(win_rows * GRID_W // CHUNK), pl.Element(width), pl.Element(CHUNK)),
                               lambda t: (win_token(t) // CHUNK, 0, 0)),
                  _resident(bias.shape)],
        out_specs=pl.BlockSpec((tile, width), lambda t: (t, 0)),
        out_shape=jax.ShapeDtypeStruct(qb.shape, _BF),
        compiler_params=_params(),
        name="nbr_attn",
    )(qb, kb, vtb, bias)


def _nbr_bias(rpb):
    variants = [(-4, 0, -5, 1), (0, 0, -1, 0), (-2, 0, -3, 0), (-6, 2, -7, 2), (-8, 2, -9, 2)]
    n_heads, n_dr, n_dc = rpb.shape
    n_kk = NBR_KEY_ROWS
    assert all(a2 == a1 - 1 for a1, _, a2, _ in variants)
    kc, c = np.meshgrid(np.arange(GRID_W), np.arange(GRID_W), indexing="ij")
    col_start = np.clip(c - B_KW // 2, 0, GRID_W - B_KW)
    col_ok = (kc >= col_start) & (kc < col_start + B_KW)
    spread = np.zeros((2, n_dc, GRID_W, 2, GRID_W), np.float32)
    for rs in range(2):
        for j in range(n_dc):
            spread[rs, j, :, rs, :] = (kc - c + (B_KW - 1)) == j
    spread = spread.reshape(2 * n_dc, GRID_W, 2 * GRID_W)
    n_blocks = max(a1 for a1, *_ in variants) + B_KH + n_kk
    rpb = rpb.astype(_F32)
    lhs = jnp.concatenate([jnp.pad(rpb, ((0, 0), (1 + rs, n_blocks - n_dr - 1 - rs), (0, 0))) for rs in range(2)],
                          axis=-1)
    blocks = jnp.einsum("hix,xkl->hikl", lhs, spread, precision=lax.Precision.HIGHEST)
    slabs = []
    for a1, b1, _, b2 in variants:
        kk = np.arange(n_kk)[:, None, None]
        row_ok = np.concatenate([np.broadcast_to((kk >= b) & (kk < b + B_KH), (n_kk, GRID_W, GRID_W))
                                 for b in (b1, b2)], axis=-1)
        ok = row_ok & np.tile(col_ok, (1, 2))[None]
        slabs.append(jnp.where(ok[None], blocks[:, a1 + B_KH:a1 + B_KH + n_kk], NEG))
    return jnp.stack(slabs, axis=1).reshape(n_heads, len(variants), n_kk * GRID_W, 2 * GRID_W)


def _out_mlp_kernel(x_ref, ya_ref, yb_ref, gate_ref, wpa_ref, wpb_ref, wo_ref, g2_ref, wup_ref, wdn_ref,
                    gf_ref, o_ref, *, final_norm):
    d = x_ref.shape[1]
    pa = jnp.dot(ya_ref[...], wpa_ref[...], preferred_element_type=_F32)
    pb = jnp.dot(yb_ref[...], wpb_ref[...], preferred_element_type=_F32)
    merged = gate_ref[:, :d].astype(_F32) * pa + gate_ref[:, d:].astype(_F32) * pb
    x1 = x_ref[...] + jnp.dot(merged.astype(_BF), wo_ref[...], preferred_element_type=_F32)
    h2 = _rms(x1, g2_ref[...]).astype(_BF)
    acc = x1
    d_ff = wup_ref.shape[1]
    for c in range(0, d_ff, FF_CHUNK):
        u = jnp.dot(h2, wup_ref[:, c:c + FF_CHUNK], preferred_element_type=_F32)
        a = jnp.square(jnp.maximum(u, 0.0)).astype(_BF)
        acc = acc + jnp.dot(a, wdn_ref[c:c + FF_CHUNK, :], preferred_element_type=_F32)
    o_ref[...] = _rms(acc, gf_ref[...]) if final_norm else acc


def _out_mlp(xf, ya, yb, gates, wpa, wpb, wo, g2, wup, wdn, gf, final_norm):
    n, d = xf.shape
    tm = min(TOKEN_TILE, n)
    row = lambda width: pl.BlockSpec((tm, width), lambda t: (t, 0))
    return pl.pallas_call(
        functools.partial(_out_mlp_kernel, final_norm=final_norm),
        grid=(n // tm,),
        in_specs=[row(d), row(ya.shape[1]), row(yb.shape[1]), row(gates.shape[1]),
                  _resident(wpa.shape), _resident(wpb.shape), _resident(wo.shape), _resident((1, d)),
                  _resident(wup.shape), _resident(wdn.shape), _resident((1, d))],
        out_specs=row(d),
        out_shape=jax.ShapeDtypeStruct((n, d), _F32),
        compiler_params=_params(),
        name="out_mlp",
    )(xf, ya, yb, gates, wpa, wpb, wo, g2, wup, wdn, gf)


def kernel(x, norm_mix, w_in, b_gate, sink, rpb, w_proj_a, w_proj_b, w_out, norm_mlp, w_up, w_down, norm_final):
    b, s, d = x.shape
    depth = w_in.shape[0]
    a_heads = sink.shape[1]
    b_heads = rpb.shape[1]
    a_w, kv_w, b_w = a_heads * HEAD_DIM, A_KV_HEADS * HEAD_DIM, b_heads * HEAD_DIM
    assert kv_w == LANES
    scale = LOG2E * HEAD_DIM ** -0.5
    win_bias = _win_bias(a_heads)
    xf = x.reshape(b * s, d)
    for l in range(depth):
        w = w_in[l]
        ka0, va0, qb0 = a_w, a_w + kv_w, a_w + 2 * kv_w
        kb0, vb0, g0 = qb0 + b_w, qb0 + 2 * b_w, qb0 + 3 * b_w
        col_scale = np.ones((1, w.shape[1]), np.float32)
        col_scale[:, :a_w] = scale
        col_scale[:, qb0:kb0] = scale
        col_of = {"qa": 0, "ka": ka0, "va": va0, "qb": qb0, "kb": kb0, "vb": vb0, "gate": g0}
        w_vt = _transposed_columns(w, [va0 + c for c in range(0, kv_w, LANES)]
                                   + [vb0 + c for c in range(0, b_w, LANES)])
        qa, ka, qb, kb, gates, vta, vtb = _in_proj(
            xf, norm_mix[l][None], (w * col_scale).astype(_BF), w_vt, b_gate[l][None], col_of,
            (a_w, 2 * kv_w, b_w, b_w, 2 * d), (kv_w, b_w))
        ya = _win_attn(LOG2E * sink[l].astype(_F32), qa, ka, vta, win_bias, s)
        yb = _nbr_attn(qb, kb, vtb, _nbr_bias(LOG2E * rpb[l].astype(_F32)), s)
        xf = _out_mlp(xf, ya, yb, gates, w_proj_a[l].astype(_BF), w_proj_b[l].astype(_BF), w_out[l].astype(_BF),
                      norm_mlp[l][None], w_up[l].astype(_BF), w_down[l].astype(_BF), norm_final[None],
                      final_norm=(l == depth - 1))
    return xf.reshape(b, s, d)
```

```python
import functools

import numpy as np
import jax
import jax.numpy as jnp
from jax import lax
from jax.experimental import pallas as pl
from jax.experimental.pallas import tpu as pltpu

HEAD_DIM = 64
LANES = 128
CHUNK = 128
A_KV_HEADS = 2
A_WINDOW = 128
GRID_W = 64
B_KH = 8
B_KW = 16
NBR_KEY_ROWS = 10
EPS = 1e-6
NEG = -1e30
LOG2E = 1.4426950408889634
ONES_ROWS = 16
PROBS_LAG = 3
VALUES_LAG = 7

IN_TILE = 1024
TOKEN_TILE = 512
WIN_TILE = 2048
NBR_ROWS = 32
FF_CHUNK = 512
VMEM_LIMIT = 56 * 1024 * 1024

_BF = jnp.bfloat16
_F32 = jnp.float32
_NT = (((1,), (1,)), ((), ()))


def _rms(x, g):
    var = jnp.mean(x * x, axis=-1, keepdims=True)
    return (x * lax.rsqrt(var + EPS)) * g


def _resident(shape):
    zeros = (0,) * len(shape)
    return pl.BlockSpec(shape, lambda t: zeros, pipeline_mode=pl.Buffered(1))


def _params():
    return pltpu.CompilerParams(dimension_semantics=("arbitrary",), vmem_limit_bytes=VMEM_LIMIT)


def _in_proj_kernel(x_ref, g_ref, w_ref, wvt_ref, bg_ref, qa_ref, ka_ref, qb_ref, kb_ref, gate_ref, vta_ref,
                    vtb_ref, *, col_of):
    hb = _rms(x_ref[...], g_ref[...]).astype(_BF)

    def proj(name, width):
        off = col_of[name]
        return jnp.dot(hb, w_ref[:, off:off + width], preferred_element_type=_F32)

    qa_ref[...] = proj("qa", qa_ref.shape[1]).astype(_BF)
    qb_ref[...] = proj("qb", qb_ref.shape[1]).astype(_BF)
    kb_ref[...] = proj("kb", kb_ref.shape[1]).astype(_BF)
    k = proj("ka", LANES)
    k_swapped = pltpu.roll(k, HEAD_DIM, axis=1)
    lo = lax.broadcasted_iota(jnp.int32, k.shape, 1) < HEAD_DIM
    ka_ref[:, :LANES] = jnp.where(lo, k, k_swapped).astype(_BF)
    ka_ref[:, LANES:] = jnp.where(lo, k_swapped, k).astype(_BF)
    gate_w = gate_ref.shape[1]
    step = 512
    for c in range(0, gate_w, step):
        g = jnp.dot(hb, w_ref[:, col_of["gate"] + c:col_of["gate"] + c + step], preferred_element_type=_F32)
        gate_ref[:, c:c + step] = jax.nn.sigmoid(g + bg_ref[:, c:c + step]).astype(_BF)
    vt = lax.dot_general(wvt_ref[...], hb, _NT, preferred_element_type=_F32).astype(_BF)
    va = vta_ref.shape[1]
    for c in range(vta_ref.shape[0]):
        vta_ref[c] = vt[:va, c * CHUNK:(c + 1) * CHUNK]
        vtb_ref[c] = vt[va:, c * CHUNK:(c + 1) * CHUNK]


def _in_proj(xf, g, w, wvt, bg, col_of, widths, v_widths):
    n, d = xf.shape
    tm = min(IN_TILE, n)
    out_shape = [jax.ShapeDtypeStruct((n, wd), _BF) for wd in widths]
    out_shape += [jax.ShapeDtypeStruct((n // CHUNK, vw, CHUNK), _BF) for vw in v_widths]
    out_specs = [pl.BlockSpec((tm, wd), lambda t: (t, 0)) for wd in widths]
    out_specs += [pl.BlockSpec((tm // CHUNK, vw, CHUNK), lambda t: (t, 0, 0)) for vw in v_widths]
    return pl.pallas_call(
        functools.partial(_in_proj_kernel, col_of=col_of),
        grid=(n // tm,),
        in_specs=[pl.BlockSpec((tm, d), lambda t: (t, 0)),
                  _resident((1, d)), _resident(w.shape), _resident(wvt.shape), _resident(bg.shape)],
        out_specs=out_specs,
        out_shape=out_shape,
        compiler_params=_params(),
        name="in_proj",
    )(xf, g, w, wvt, bg)


def _transpose_kernel(starts_ref, w_ref, o_ref):
    o_ref[...] = w_ref[...].T.astype(_BF)


def _transposed_columns(w, col_starts):
    d = w.shape[0]
    assert all(c % LANES == 0 for c in col_starts)
    blocks = jnp.asarray([c // LANES for c in col_starts], jnp.int32)
    return pl.pallas_call(
        _transpose_kernel,
        grid_spec=pltpu.PrefetchScalarGridSpec(
            num_scalar_prefetch=1, grid=(len(col_starts),),
            in_specs=[pl.BlockSpec((d, LANES), lambda t, starts: (0, starts[t]))],
            out_specs=pl.BlockSpec((LANES, d), lambda t, starts: (t, 0))),
        out_shape=jax.ShapeDtypeStruct((LANES * len(col_starts), d), _BF),
        compiler_params=_params(),
        name="transpose_value_weights",
    )(blocks, w)


def _with_ones(vt):
    return jnp.concatenate([vt, jnp.ones((ONES_ROWS, vt.shape[1]), vt.dtype)], axis=0)


def _scores(k_all, qm, bias_rows, sink):
    n_keys = k_all.shape[0]
    s_all = lax.dot_general(k_all, qm, _NT, preferred_element_type=_F32)
    chunks = [s_all[lo:min(lo + CHUNK, n_keys)] + bias_rows(lo, min(lo + CHUNK, n_keys))
              for lo in range(0, n_keys, CHUNK)]
    full = [c for c in chunks if c.shape[0] == CHUNK]
    m = jnp.max(functools.reduce(jnp.maximum, full), axis=0, keepdims=True)
    for c in chunks[len(full):]:
        m = jnp.maximum(m, jnp.max(c, axis=0, keepdims=True))
    return chunks, (m if sink is None else jnp.maximum(m, sink))


def _probs(chunks, m, n_rows):
    p = [jnp.exp2(s_c - m).astype(_BF) for s_c in chunks]
    pad = n_rows - sum(c.shape[0] for c in chunks)
    if pad:
        p.append(jnp.zeros((pad, p[0].shape[1]), _BF))
    return jnp.concatenate(p, axis=0)


def _values(vt_aug, sink, m, p):
    oa = jnp.dot(vt_aug, p, preferred_element_type=_F32)
    l = oa[HEAD_DIM:HEAD_DIM + 1]
    if sink is not None:
        l = l + jnp.exp2(sink - m)
    return oa[:HEAD_DIM] * (1.0 / l)


def _pipelined(n_jobs, score_job, value_job, n_rows):
    scored, probs = {}, {}
    for step in range(n_jobs + VALUES_LAG):
        if step < n_jobs:
            scored[step] = score_job(step)
        if 0 <= step - VALUES_LAG < n_jobs:
            value_job(step - VALUES_LAG, *probs.pop(step - VALUES_LAG))
        if 0 <= step - PROBS_LAG < n_jobs:
            chunks, m = scored.pop(step - PROBS_LAG)
            probs[step - PROBS_LAG] = (m, _probs(chunks, m, n_rows))


def _one_head(q_tile, half):
    lane = lax.broadcasted_iota(jnp.int32, q_tile.shape, 1)
    keep = (lane < HEAD_DIM) if half == 0 else (lane >= HEAD_DIM)
    return jnp.where(keep, q_tile, jnp.zeros_like(q_tile))


def _win_attn_kernel(sink_ref, q_ref, kp_ref, kc_ref, kn_ref, vp_ref, vc_ref, vn_ref, bias_ref, o_ref,
                     *, tiles_per_batch):
    blk = A_WINDOW
    nsub = q_ref.shape[0] // blk
    n_pairs = q_ref.shape[1] // LANES
    i = pl.program_id(0) % tiles_per_batch
    first_neg = jnp.where(i == 0, NEG, 0.0)
    last_neg = jnp.where(i == tiles_per_batch - 1, NEG, 0.0)
    n_heads = 2 * n_pairs
    grp = n_heads // A_KV_HEADS
    pending = {}

    def key_block(b, kv):
        cols = slice(kv * LANES, (kv + 1) * LANES)
        if b < 0:
            return kp_ref[:, cols]
        return kn_ref[:, cols] if b == nsub else kc_ref[b * blk:(b + 1) * blk, cols]

    def score_job(n):
        j, h = divmod(n, n_heads)

        def bias_rows(lo, hi):
            b = bias_ref[h, lo:hi]
            if j == 0 and lo == 0:
                b = b + first_neg
            if j == nsub - 1 and lo == 2 * blk:
                b = b + last_neg
            return b

        q_tile = q_ref[j * blk:(j + 1) * blk, (h // 2) * LANES:(h // 2 + 1) * LANES]
        k_all = jnp.concatenate([key_block(j - 1 + c, h // grp) for c in range(3)], axis=0)
        return _scores(k_all, _one_head(q_tile, h % 2), bias_rows, sink_ref[h])

    def value_job(n, m, p):
        j, h = divmod(n, n_heads)
        kv = slice((h // grp) * HEAD_DIM, (h // grp + 1) * HEAD_DIM)
        v_t = jnp.concatenate(
            [vp_ref[0, kv] if j == 0 else vc_ref[j - 1, kv], vc_ref[j, kv],
             vn_ref[0, kv] if j == nsub - 1 else vc_ref[j + 1, kv]], axis=1)
        pending[h % 2] = _values(_with_ones(v_t), sink_ref[h], m, p)
        if h % 2 == 1:
            o_ref[j * blk:(j + 1) * blk, (h // 2) * LANES:(h // 2 + 1) * LANES] = (
                jnp.concatenate([pending[0], pending[1]], axis=0).T.astype(_BF))

    _pipelined(nsub * n_heads, score_job, value_job, 3 * blk)


def _win_attn(sink, qa, ka, vta, bias, seq):
    n = qa.shape[0]
    tq = min(WIN_TILE, seq)
    blk = A_WINDOW
    sub = tq // blk
    n_blk = n // blk
    prev_blk = lambda t: jnp.maximum(t * sub - 1, 0)
    next_blk = lambda t: jnp.minimum((t + 1) * sub, n_blk - 1)
    kernel = functools.partial(_win_attn_kernel, tiles_per_batch=seq // tq)
    kw, vw = ka.shape[1], vta.shape[1]
    return pl.pallas_call(
        kernel,
        grid=(n // tq,),
        in_specs=[pl.BlockSpec(memory_space=pltpu.SMEM),
                  pl.BlockSpec((tq, qa.shape[1]), lambda t: (t, 0)),
                  pl.BlockSpec((blk, kw), lambda t: (prev_blk(t), 0)),
                  pl.BlockSpec((tq, kw), lambda t: (t, 0)),
                  pl.BlockSpec((blk, kw), lambda t: (next_blk(t), 0)),
                  pl.BlockSpec((1, vw, CHUNK), lambda t: (prev_blk(t), 0, 0)),
                  pl.BlockSpec((sub, vw, CHUNK), lambda t: (t, 0, 0)),
                  pl.BlockSpec((1, vw, CHUNK), lambda t: (next_blk(t), 0, 0)),
                  _resident(bias.shape)],
        out_specs=pl.BlockSpec((tq, qa.shape[1]), lambda t: (t, 0)),
        out_shape=jax.ShapeDtypeStruct(qa.shape, _BF),
        compiler_params=_params(),
        name="win_attn",
    )(sink, qa, ka, ka, ka, vta, vta, vta, bias)


def _win_bias(n_heads):
    slopes = 2.0 ** (-8.0 * jnp.arange(1, n_heads + 1, dtype=_F32) / n_heads)
    k_pos = jnp.arange(3 * A_WINDOW)[:, None] - A_WINDOW
    q_pos = jnp.arange(A_WINDOW)[None, :]
    dist = jnp.abs(q_pos - k_pos)
    bias = -(LOG2E * slopes)[:, None, None] * dist.astype(_F32)[None]
    return jnp.where((dist <= A_WINDOW)[None], bias, NEG)


def _nbr_attn_kernel(q_ref, kwin_ref, vwin_ref, bias_ref, o_ref, *, rows, tiles_per_batch):
    rt = q_ref.shape[0] // GRID_W
    win_rows = kwin_ref.shape[0] // GRID_W
    n_heads = 2 * (q_ref.shape[1] // LANES)
    n_keys = NBR_KEY_ROWS * GRID_W
    n_chunks = n_keys // CHUNK
    n_units = rt // 2
    first_row = (pl.program_id(0) % tiles_per_batch) * rt
    win_row0 = jnp.clip(first_row - B_KH // 2, 0, rows - win_rows)
    pending = {}

    def key_row0(jp):
        r = first_row + 2 * jp
        return jnp.clip(r - B_KH // 2, 0, rows - NBR_KEY_ROWS) - win_row0

    def score_job(n):
        jp, h = divmod(n, n_heads)
        r = first_row + 2 * jp
        k_off = pl.multiple_of(key_row0(jp) * GRID_W, CHUNK)
        combo = jnp.where(r == 0, 1, jnp.where(r == 2, 2, jnp.where(r == rows - 4, 3,
                                                                    jnp.where(r == rows - 2, 4, 0))))
        cols = slice((h // 2) * LANES, (h // 2 + 1) * LANES)
        q_tile = q_ref[jp * 2 * GRID_W:(jp + 1) * 2 * GRID_W, cols]
        job_keys = n_keys if jp >= n_units - 2 else n_keys - GRID_W
        return _scores(kwin_ref[pl.ds(k_off, job_keys), cols], _one_head(q_tile, h % 2),
                       lambda lo, hi: bias_ref[h, combo, lo:hi], None)

    def value_job(n, m, p):
        jp, h = divmod(n, n_heads)
        c_off = key_row0(jp) // 2
        v_t = jnp.concatenate([vwin_ref[c_off + c, h * HEAD_DIM:(h + 1) * HEAD_DIM] for c in range(n_chunks)],
                              axis=1)
        pending[h % 2] = _values(_with_ones(v_t), None, m, p)
        if h % 2 == 1:
            o_ref[jp * 2 * GRID_W:(jp + 1) * 2 * GRID_W, (h // 2) * LANES:(h // 2 + 1) * LANES] = (
                jnp.concatenate([pending[0], pending[1]], axis=0).T.astype(_BF))

    _pipelined(n_units * n_heads, score_job, value_job, n_keys)


def _nbr_attn(qb, kb, vtb, bias, seq):
    n, width = qb.shape
    rows = seq // GRID_W
    rt = NBR_ROWS
    win_rows = rt + B_KH
    assert rows >= win_rows and rows % rt == 0 and win_rows % 2 == 0
    tile = rt * GRID_W
    tiles_per_batch = seq // tile
    kernel = functools.partial(_nbr_attn_kernel, rows=rows, tiles_per_batch=tiles_per_batch)

    def win_token(t):
        first_row = (t % tiles_per_batch) * rt
        win_row0 = jnp.clip(first_row - B_KH // 2, 0, rows - win_rows)
        return (t // tiles_per_batch) * seq + win_row0 * GRID_W

    return pl.pallas_call(
        kernel,
        grid=(n // tile,),
        in_specs=[pl.BlockSpec((tile, width), lambda t: (t, 0)),
                  pl.BlockSpec((pl.Element(win_rows * GRID_W), pl.Element(width)),
                               lambda t: (pl.multiple_of(win_token(t), CHUNK), 0)),
                  pl.BlockSpec((pl.Element(win_rows * GRID_W // CHUNK), pl.Element(width), pl.Element(CHUNK)),
                               lambda t: (win_token(t) // CHUNK, 0, 0)),
                  _resident(bias.shape)],
        out_specs=pl.BlockSpec((tile, width), lambda t: (t, 0)),
        out_shape=jax.ShapeDtypeStruct(qb.shape, _BF),
        compiler_params=_params(),
        name="nbr_attn",
    )(qb, kb, vtb, bias)


def _nbr_bias(rpb):
    variants = [(-4, 0, -5, 1), (0, 0, -1, 0), (-2, 0, -3, 0), (-6, 2, -7, 2), (-8, 2, -9, 2)]
    n_heads, n_dr, n_dc = rpb.shape
    n_kk = NBR_KEY_ROWS
    assert all(a2 == a1 - 1 for a1, _, a2, _ in variants)
    kc, c = np.meshgrid(np.arange(GRID_W), np.arange(GRID_W), indexing="ij")
    col_start = np.clip(c - B_KW // 2, 0, GRID_W - B_KW)
    col_ok = (kc >= col_start) & (kc < col_start + B_KW)
    spread = np.zeros((2, n_dc, GRID_W, 2, GRID_W), np.float32)
    for rs in range(2):
        for j in range(n_dc):
            spread[rs, j, :, rs, :] = (kc - c + (B_KW - 1)) == j
    spread = spread.reshape(2 * n_dc, GRID_W, 2 * GRID_W)
    n_blocks = max(a1 for a1, *_ in variants) + B_KH + n_kk
    rpb = rpb.astype(_F32)
    lhs = jnp.concatenate([jnp.pad(rpb, ((0, 0), (1 + rs, n_blocks - n_dr - 1 - rs), (0, 0))) for rs in range(2)],
                          axis=-1)
    blocks = jnp.einsum("hix,xkl->hikl", lhs, spread, precision=lax.Precision.HIGHEST)
    slabs = []
    for a1, b1, _, b2 in variants:
        kk = np.arange(n_kk)[:, None, None]
        row_ok = np.concatenate([np.broadcast_to((kk >= b) & (kk < b + B_KH), (n_kk, GRID_W, GRID_W))
                                 for b in (b1, b2)], axis=-1)
        ok = row_ok & np.tile(col_ok, (1, 2))[None]
        slabs.append(jnp.where(ok[None], blocks[:, a1 + B_KH:a1 + B_KH + n_kk], NEG))
    return jnp.stack(slabs, axis=1).reshape(n_heads, len(variants), n_kk * GRID_W, 2 * GRID_W)


def _out_mlp_kernel(x_ref, ya_ref, yb_ref, gate_ref, wpa_ref, wpb_ref, wo_ref, g2_ref, wup_ref, wdn_ref,
                    gf_ref, o_ref, *, final_norm):
    d = x_ref.shape[1]
    pa = jnp.dot(ya_ref[...], wpa_ref[...], preferred_element_type=_F32)
    pb = jnp.dot(yb_ref[...], wpb_ref[...], preferred_element_type=_F32)
    merged = gate_ref[:, :d].astype(_F32) * pa + gate_ref[:, d:].astype(_F32) * pb
    x1 = x_ref[...] + jnp.dot(merged.astype(_BF), wo_ref[...], preferred_element_type=_F32)
    h2 = _rms(x1, g2_ref[...]).astype(_BF)
    acc = x1
    d_ff = wup_ref.shape[1]
    for c in range(0, d_ff, FF_CHUNK):
        u = jnp.dot(h2, wup_ref[:, c:c + FF_CHUNK], preferred_element_type=_F32)
        a = jnp.square(jnp.maximum(u, 0.0)).astype(_BF)
        acc = acc + jnp.dot(a, wdn_ref[c:c + FF_CHUNK, :], preferred_element_type=_F32)
    o_ref[...] = _rms(acc, gf_ref[...]) if final_norm else acc


def _out_mlp(xf, ya, yb, gates, wpa, wpb, wo, g2, wup, wdn, gf, final_norm):
    n, d = xf.shape
    tm = min(TOKEN_TILE, n)
    row = lambda width: pl.BlockSpec((tm, width), lambda t: (t, 0))
    return pl.pallas_call(
        functools.partial(_out_mlp_kernel, final_norm=final_norm),
        grid=(n // tm,),
        in_specs=[row(d), row(ya.shape[1]), row(yb.shape[1]), row(gates.shape[1]),
                  _resident(wpa.shape), _resident(wpb.shape), _resident(wo.shape), _resident((1, d)),
                  _resident(wup.shape), _resident(wdn.shape), _resident((1, d))],
        out_specs=row(d),
        out_shape=jax.ShapeDtypeStruct((n, d), _F32),
        compiler_params=_params(),
        name="out_mlp",
    )(xf, ya, yb, gates, wpa, wpb, wo, g2, wup, wdn, gf)


def kernel(x, norm_mix, w_in, b_gate, sink, rpb, w_proj_a, w_proj_b, w_out, norm_mlp, w_up, w_down, norm_final):
    b, s, d = x.shape
    depth = w_in.shape[0]
    a_heads = sink.shape[1]
    b_heads = rpb.shape[1]
    a_w, kv_w, b_w = a_heads * HEAD_DIM, A_KV_HEADS * HEAD_DIM, b_heads * HEAD_DIM
    assert kv_w == LANES
    scale = LOG2E * HEAD_DIM ** -0.5
    win_bias = _win_bias(a_heads)
    xf = x.reshape(b * s, d)
    for l in range(depth):
        w = w_in[l]
        ka0, va0, qb0 = a_w, a_w + kv_w, a_w + 2 * kv_w
        kb0, vb0, g0 = qb0 + b_w, qb0 + 2 * b_w, qb0 + 3 * b_w
        col_scale = np.ones((1, w.shape[1]), np.float32)
        col_scale[:, :a_w] = scale
        col_scale[:, qb0:kb0] = scale
        col_of = {"qa": 0, "ka": ka0, "va": va0, "qb": qb0, "kb": kb0, "vb": vb0, "gate": g0}
        w_vt = _transposed_columns(w, [va0 + c for c in range(0, kv_w, LANES)]
                                   + [vb0 + c for c in range(0, b_w, LANES)])
        qa, ka, qb, kb, gates, vta, vtb = _in_proj(
            xf, norm_mix[l][None], (w * col_scale).astype(_BF), w_vt, b_gate[l][None], col_of,
            (a_w, 2 * kv_w, b_w, b_w, 2 * d), (kv_w, b_w))
        ya = _win_attn(LOG2E * sink[l].astype(_F32), qa, ka, vta, win_bias, s)
        yb = _nbr_attn(qb, kb, vtb, _nbr_bias(LOG2E * rpb[l].astype(_F32)), s)
        xf = _out_mlp(xf, ya, yb, gates, w_proj_a[l].astype(_BF), w_proj_b[l].astype(_BF), w_out[l].astype(_BF),
                      norm_mlp[l][None], w_up[l].astype(_BF), w_down[l].astype(_BF), norm_final[None],
                      final_norm=(l == depth - 1))
    return xf.reshape(b, s, d)
```

```python
import functools

import numpy as np
import jax
import jax.numpy as jnp
from jax import lax
from jax.experimental import pallas as pl
from jax.experimental.pallas import tpu as pltpu

HEAD_DIM = 64
LANES = 128
CHUNK = 128
A_KV_HEADS = 2
A_WINDOW = 128
GRID_W = 64
B_KH = 8
B_KW = 16
NBR_KEY_ROWS = 10
EPS = 1e-6
NEG = -1e30
LOG2E = 1.4426950408889634
ONES_ROWS = 16
PROBS_LAG = 3
VALUES_LAG = 7

IN_TILE = 1024
GATE_CHUNK = 512
TOKEN_TILE = 512
WIN_TILE = 2048
NBR_ROWS = 32
FF_CHUNK = 512
VMEM_LIMIT = 56 * 1024 * 1024

_BF = jnp.bfloat16
_F32 = jnp.float32
_NT = (((1,), (1,)), ((), ()))


def _rms(x, g):
    var = jnp.mean(x * x, axis=-1, keepdims=True)
    return (x * lax.rsqrt(var + EPS)) * g


def _resident(shape):
    zeros = (0,) * len(shape)
    return pl.BlockSpec(shape, lambda t: zeros, pipeline_mode=pl.Buffered(1))


def _params():
    return pltpu.CompilerParams(dimension_semantics=("arbitrary",), vmem_limit_bytes=VMEM_LIMIT)


def _in_proj_kernel(x_ref, g_ref, w_ref, bg_ref, qa_ref, ka_ref, qb_ref, kb_ref, gate_ref, vta_ref,
                    vtb_ref, wvt_ref, *, col_of):
    va = vta_ref.shape[1]

    @pl.when(pl.program_id(0) == 0)
    def _():
        for r in range(0, wvt_ref.shape[0], LANES):
            off = col_of["va"] + r if r < va else col_of["vb"] + r - va
            wvt_ref[r:r + LANES, :] = w_ref[:, off:off + LANES].astype(_F32).T.astype(_BF)

    hb = _rms(x_ref[...], g_ref[...]).astype(_BF)

    def proj(name, width):
        off = col_of[name]
        return jnp.dot(hb, w_ref[:, off:off + width], preferred_element_type=_F32)

    qa_ref[...] = proj("qa", qa_ref.shape[1]).astype(_BF)
    qb_ref[...] = proj("qb", qb_ref.shape[1]).astype(_BF)
    kb_ref[...] = proj("kb", kb_ref.shape[1]).astype(_BF)
    k = proj("ka", LANES)
    k_swapped = pltpu.roll(k, HEAD_DIM, axis=1)
    lo = lax.broadcasted_iota(jnp.int32, k.shape, 1) < HEAD_DIM
    ka_ref[:, :LANES] = jnp.where(lo, k, k_swapped).astype(_BF)
    ka_ref[:, LANES:] = jnp.where(lo, k_swapped, k).astype(_BF)
    for c in range(0, gate_ref.shape[1], GATE_CHUNK):
        off = col_of["gate"] + c
        g = jnp.dot(hb, w_ref[:, off:off + GATE_CHUNK], preferred_element_type=_F32)
        gate_ref[:, c:c + GATE_CHUNK] = jax.nn.sigmoid(g + bg_ref[:, c:c + GATE_CHUNK]).astype(_BF)
    vt = lax.dot_general(wvt_ref[...], hb, _NT, preferred_element_type=_F32).astype(_BF)
    for c in range(vta_ref.shape[0]):
        vta_ref[c] = vt[:va, c * CHUNK:(c + 1) * CHUNK]
        vtb_ref[c] = vt[va:, c * CHUNK:(c + 1) * CHUNK]


def _in_proj(xf, g, w, bg, col_of, widths, v_widths):
    n, d = xf.shape
    assert all(vw % LANES == 0 for vw in v_widths)
    tm = min(IN_TILE, n)
    out_shape = [jax.ShapeDtypeStruct((n, wd), _BF) for wd in widths]
    out_shape += [jax.ShapeDtypeStruct((n // CHUNK, vw, CHUNK), _BF) for vw in v_widths]
    out_specs = [pl.BlockSpec((tm, wd), lambda t: (t, 0)) for wd in widths]
    out_specs += [pl.BlockSpec((tm // CHUNK, vw, CHUNK), lambda t: (t, 0, 0)) for vw in v_widths]
    return pl.pallas_call(
        functools.partial(_in_proj_kernel, col_of=col_of),
        grid=(n // tm,),
        in_specs=[pl.BlockSpec((tm, d), lambda t: (t, 0)),
                  _resident((1, d)), _resident(w.shape), _resident(bg.shape)],
        out_specs=out_specs,
        out_shape=out_shape,
        scratch_shapes=[pltpu.VMEM((sum(v_widths), d), _BF)],
        compiler_params=_params(),
        name="in_proj",
    )(xf, g, w, bg)


def _with_ones(vt):
    return jnp.concatenate([vt, jnp.ones((ONES_ROWS, vt.shape[1]), vt.dtype)], axis=0)


def _scores(k_all, qm, bias_rows, sink):
    n_keys = k_all.shape[0]
    s_all = lax.dot_general(k_all, qm, _NT, preferred_element_type=_F32)
    chunks = [s_all[lo:min(lo + CHUNK, n_keys)] + bias_rows(lo, min(lo + CHUNK, n_keys))
              for lo in range(0, n_keys, CHUNK)]
    full = [c for c in chunks if c.shape[0] == CHUNK]
    m = jnp.max(functools.reduce(jnp.maximum, full), axis=0, keepdims=True)
    for c in chunks[len(full):]:
        m = jnp.maximum(m, jnp.max(c, axis=0, keepdims=True))
    return chunks, (m if sink is None else jnp.maximum(m, sink))


def _probs(chunks, m, n_rows):
    p = [jnp.exp2(s_c - m).astype(_BF) for s_c in chunks]
    pad = n_rows - sum(c.shape[0] for c in chunks)
    if pad:
        p.append(jnp.zeros((pad, p[0].shape[1]), _BF))
    return jnp.concatenate(p, axis=0)


def _values(vt_aug, sink, m, p):
    oa = jnp.dot(vt_aug, p, preferred_element_type=_F32)
    l = oa[HEAD_DIM:HEAD_DIM + 1]
    if sink is not None:
        l = l + jnp.exp2(sink - m)
    return oa[:HEAD_DIM] * (1.0 / l)


def _pipelined(n_jobs, score_job, value_job, n_rows):
    scored, probs = {}, {}
    for step in range(n_jobs + VALUES_LAG):
        if step < n_jobs:
            scored[step] = score_job(step)
        if 0 <= step - VALUES_LAG < n_jobs:
            value_job(step - VALUES_LAG, *probs.pop(step - VALUES_LAG))
        if 0 <= step - PROBS_LAG < n_jobs:
            chunks, m = scored.pop(step - PROBS_LAG)
            probs[step - PROBS_LAG] = (m, _probs(chunks, m, n_rows))


def _one_head(q_tile, half):
    lane = lax.broadcasted_iota(jnp.int32, q_tile.shape, 1)
    keep = (lane < HEAD_DIM) if half == 0 else (lane >= HEAD_DIM)
    return jnp.where(keep, q_tile, jnp.zeros_like(q_tile))


def _win_attn_kernel(sink_ref, q_ref, kp_ref, kc_ref, kn_ref, vp_ref, vc_ref, vn_ref, bias_ref, o_ref,
                     *, tiles_per_batch):
    blk = A_WINDOW
    nsub = q_ref.shape[0] // blk
    n_pairs = q_ref.shape[1] // LANES
    i = pl.program_id(0) % tiles_per_batch
    first_neg = jnp.where(i == 0, NEG, 0.0)
    last_neg = jnp.where(i == tiles_per_batch - 1, NEG, 0.0)
    n_heads = 2 * n_pairs
    grp = n_heads // A_KV_HEADS
    pending = {}

    def key_block(b, kv):
        cols = slice(kv * LANES, (kv + 1) * LANES)
        if b < 0:
            return kp_ref[:, cols]
        return kn_ref[:, cols] if b == nsub else kc_ref[b * blk:(b + 1) * blk, cols]

    def score_job(n):
        j, h = divmod(n, n_heads)

        def bias_rows(lo, hi):
            b = bias_ref[h, lo:hi]
            if j == 0 and lo == 0:
                b = b + first_neg
            if j == nsub - 1 and lo == 2 * blk:
                b = b + last_neg
            return b

        q_tile = q_ref[j * blk:(j + 1) * blk, (h // 2) * LANES:(h // 2 + 1) * LANES]
        k_all = jnp.concatenate([key_block(j - 1 + c, h // grp) for c in range(3)], axis=0)
        return _scores(k_all, _one_head(q_tile, h % 2), bias_rows, sink_ref[h])

    def value_job(n, m, p):
        j, h = divmod(n, n_heads)
        kv = slice((h // grp) * HEAD_DIM, (h // grp + 1) * HEAD_DIM)
        v_t = jnp.concatenate(
            [vp_ref[0, kv] if j == 0 else vc_ref[j - 1, kv], vc_ref[j, kv],
             vn_ref[0, kv] if j == nsub - 1 else vc_ref[j + 1, kv]], axis=1)
        pending[h % 2] = _values(_with_ones(v_t), sink_ref[h], m, p)
        if h % 2 == 1:
            o_ref[j * blk:(j + 1) * blk, (h // 2) * LANES:(h // 2 + 1) * LANES] = (
                jnp.concatenate([pending[0], pending[1]], axis=0).T.astype(_BF))

    _pipelined(nsub * n_heads, score_job, value_job, 3 * blk)


def _win_attn(sink, qa, ka, vta, bias, seq):
    n = qa.shape[0]
    tq = min(WIN_TILE, seq)
    blk = A_WINDOW
    sub = tq // blk
    n_blk = n // blk
    prev_blk = lambda t: jnp.maximum(t * sub - 1, 0)
    next_blk = lambda t: jnp.minimum((t + 1) * sub, n_blk - 1)
    kernel = functools.partial(_win_attn_kernel, tiles_per_batch=seq // tq)
    kw, vw = ka.shape[1], vta.shape[1]
    return pl.pallas_call(
        kernel,
        grid=(n // tq,),
        in_specs=[pl.BlockSpec(memory_space=pltpu.SMEM),
                  pl.BlockSpec((tq, qa.shape[1]), lambda t: (t, 0)),
                  pl.BlockSpec((blk, kw), lambda t: (prev_blk(t), 0)),
                  pl.BlockSpec((tq, kw), lambda t: (t, 0)),
                  pl.BlockSpec((blk, kw), lambda t: (next_blk(t), 0)),
                  pl.BlockSpec((1, vw, CHUNK), lambda t: (prev_blk(t), 0, 0)),
                  pl.BlockSpec((sub, vw, CHUNK), lambda t: (t, 0, 0)),
                  pl.BlockSpec((1, vw, CHUNK), lambda t: (next_blk(t), 0, 0)),
                  _resident(bias.shape)],
        out_specs=pl.BlockSpec((tq, qa.shape[1]), lambda t: (t, 0)),
        out_shape=jax.ShapeDtypeStruct(qa.shape, _BF),
        compiler_params=_params(),
        name="win_attn",
    )(sink, qa, ka, ka, ka, vta, vta, vta, bias)


def _win_bias(n_heads):
    slopes = 2.0 ** (-8.0 * jnp.arange(1, n_heads + 1, dtype=_F32) / n_heads)
    k_pos = jnp.arange(3 * A_WINDOW)[:, None] - A_WINDOW
    q_pos = jnp.arange(A_WINDOW)[None, :]
    dist = jnp.abs(q_pos - k_pos)
    bias = -(LOG2E * slopes)[:, None, None] * dist.astype(_F32)[None]
    return jnp.where((dist <= A_WINDOW)[None], bias, NEG)


def _nbr_attn_kernel(q_ref, kwin_ref, vwin_ref, bias_ref, o_ref, *, rows, tiles_per_batch):
    rt = q_ref.shape[0] // GRID_W
    win_rows = kwin_ref.shape[0] // GRID_W
    n_heads = 2 * (q_ref.shape[1] // LANES)
    n_keys = NBR_KEY_ROWS * GRID_W
    n_chunks = n_keys // CHUNK
    n_units = rt // 2
    first_row = (pl.program_id(0) % tiles_per_batch) * rt
    win_row0 = jnp.clip(first_row - B_KH // 2, 0, rows - win_rows)
    pending = {}

    def key_row0(jp):
        r = first_row + 2 * jp
        return jnp.clip(r - B_KH // 2, 0, rows - NBR_KEY_ROWS) - win_row0

    def score_job(n):
        jp, h = divmod(n, n_heads)
        r = first_row + 2 * jp
        k_off = pl.multiple_of(key_row0(jp) * GRID_W, CHUNK)
        combo = jnp.where(r == 0, 1, jnp.where(r == 2, 2, jnp.where(r == rows - 4, 3,
                                                                    jnp.where(r == rows - 2, 4, 0))))
        cols = slice((h // 2) * LANES, (h // 2 + 1) * LANES)
        q_tile = q_ref[jp * 2 * GRID_W:(jp + 1) * 2 * GRID_W, cols]
        job_keys = n_keys if jp >= n_units - 2 else n_keys - GRID_W
        return _scores(kwin_ref[pl.ds(k_off, job_keys), cols], _one_head(q_tile, h % 2),
                       lambda lo, hi: bias_ref[h, combo, lo:hi], None)

    def value_job(n, m, p):
        jp, h = divmod(n, n_heads)
        c_off = key_row0(jp) // 2
        v_t = jnp.concatenate([vwin_ref[c_off + c, h * HEAD_DIM:(h + 1) * HEAD_DIM] for c in range(n_chunks)],
                              axis=1)
        pending[h % 2] = _values(_with_ones(v_t), None, m, p)
        if h % 2 == 1:
            o_ref[jp * 2 * GRID_W:(jp + 1) * 2 * GRID_W, (h // 2) * LANES:(h // 2 + 1) * LANES] = (
                jnp.concatenate([pending[0], pending[1]], axis=0).T.astype(_BF))

    _pipelined(n_units * n_heads, score_job, value_job, n_keys)


def _nbr_attn(qb, kb, vtb, bias, seq):
    n, width = qb.shape
    rows = seq // GRID_W
    rt = NBR_ROWS
    win_rows = rt + B_KH
    assert rows >= win_rows and rows % rt == 0 and win_rows % 2 == 0
    tile = rt * GRID_W
    tiles_per_batch = seq // tile
    kernel = functools.partial(_nbr_attn_kernel, rows=rows, tiles_per_batch=tiles_per_batch)

    def win_token(t):
        first_row = (t % tiles_per_batch) * rt
        win_row0 = jnp.clip(first_row - B_KH // 2, 0, rows - win_rows)
        return (t // tiles_per_batch) * seq + win_row0 * GRID_W

    return pl.pallas_call(
        kernel,
        grid=(n // tile,),
        in_specs=[pl.BlockSpec((tile, width), lambda t: (t, 0)),
                  pl.BlockSpec((pl.Element(win_rows * GRID_W), pl.Element(width)),
                               lambda t: (pl.multiple_of(win_token(t), CHUNK), 0)),
                  pl.BlockSpec((pl.Element(win_rows * GRID_W // CHUNK), pl.Element(width), pl.Element(CHUNK)),
                               lambda t: (win_token(t) // CHUNK, 0, 0)),
                  _resident(bias.shape)],
        out_specs=pl.BlockSpec((tile, width), lambda t: (t, 0)),
        out_shape=jax.ShapeDtypeStruct(qb.shape, _BF),
        compiler_params=_params(),
        name="nbr_attn",
    )(qb, kb, vtb, bias)


def _nbr_bias(rpb):
    variants = [(-4, 0, -5, 1), (0, 0, -1, 0), (-2, 0, -3, 0), (-6, 2, -7, 2), (-8, 2, -9, 2)]
    n_heads, n_dr, n_dc = rpb.shape
    n_kk = NBR_KEY_ROWS
    assert all(a2 == a1 - 1 for a1, _, a2, _ in variants)
    kc, c = np.meshgrid(np.arange(GRID_W), np.arange(GRID_W), indexing="ij")
    col_start = np.clip(c - B_KW // 2, 0, GRID_W - B_KW)
    col_ok = (kc >= col_start) & (kc < col_start + B_KW)
    spread = np.zeros((2, n_dc, GRID_W, 2, GRID_W), np.float32)
    for rs in range(2):
        for j in range(n_dc):
            spread[rs, j, :, rs, :] = (kc - c + (B_KW - 1)) == j
    spread = spread.reshape(2 * n_dc, GRID_W, 2 * GRID_W)
    n_blocks = max(a1 for a1, *_ in variants) + B_KH + n_kk
    rpb = rpb.astype(_F32)
    lhs = jnp.concatenate([jnp.pad(rpb, ((0, 0), (1 + rs, n_blocks - n_dr - 1 - rs), (0, 0))) for rs in range(2)],
                          axis=-1)
    blocks = jnp.einsum("hix,xkl->hikl", lhs, spread, precision=lax.Precision.HIGHEST)
    slabs = []
    for a1, b1, _, b2 in variants:
        kk = np.arange(n_kk)[:, None, None]
        row_ok = np.concatenate([np.broadcast_to((kk >= b) & (kk < b + B_KH), (n_kk, GRID_W, GRID_W))
                                 for b in (b1, b2)], axis=-1)
        ok = row_ok & np.tile(col_ok, (1, 2))[None]
        slabs.append(jnp.where(ok[None], blocks[:, a1 + B_KH:a1 + B_KH + n_kk], NEG))
    return jnp.stack(slabs, axis=1).reshape(n_heads, len(variants), n_kk * GRID_W, 2 * GRID_W)


def _out_mlp_kernel(x_ref, ya_ref, yb_ref, gate_ref, wpa_ref, wpb_ref, wo_ref, g2_ref, wup_ref, wdn_ref,
                    gf_ref, o_ref, *, final_norm):
    d = x_ref.shape[1]
    pa = jnp.dot(ya_ref[...], wpa_ref[...], preferred_element_type=_F32)
    pb = jnp.dot(yb_ref[...], wpb_ref[...], preferred_element_type=_F32)
    merged = gate_ref[:, :d].astype(_F32) * pa + gate_ref[:, d:].astype(_F32) * pb
    x1 = x_ref[...] + jnp.dot(merged.astype(_BF), wo_ref[...], preferred_element_type=_F32)
    h2 = _rms(x1, g2_ref[...]).astype(_BF)
    acc = x1
    d_ff = wup_ref.shape[1]
    for c in range(0, d_ff, FF_CHUNK):
        u = jnp.dot(h2, wup_ref[:, c:c + FF_CHUNK], preferred_element_type=_F32)
        a = jnp.square(jnp.maximum(u, 0.0)).astype(_BF)
        acc = acc + jnp.dot(a, wdn_ref[c:c + FF_CHUNK, :], preferred_element_type=_F32)
    o_ref[...] = _rms(acc, gf_ref[...]) if final_norm else acc


def _out_mlp(xf, ya, yb, gates, wpa, wpb, wo, g2, wup, wdn, gf, final_norm):
    n, d = xf.shape
    tm = min(TOKEN_TILE, n)
    row = lambda width: pl.BlockSpec((tm, width), lambda t: (t, 0))
    return pl.pallas_call(
        functools.partial(_out_mlp_kernel, final_norm=final_norm),
        grid=(n // tm,),
        in_specs=[row(d), row(ya.shape[1]), row(yb.shape[1]), row(gates.shape[1]),
                  _resident(wpa.shape), _resident(wpb.shape), _resident(wo.shape), _resident((1, d)),
                  _resident(wup.shape), _resident(wdn.shape), _resident((1, d))],
        out_specs=row(d),
        out_shape=jax.ShapeDtypeStruct((n, d), _F32),
        compiler_params=_params(),
        name="out_mlp",
    )(xf, ya, yb, gates, wpa, wpb, wo, g2, wup, wdn, gf)


def kernel(x, norm_mix, w_in, b_gate, sink, rpb, w_proj_a, w_proj_b, w_out, norm_mlp, w_up, w_down, norm_final):
    b, s, d = x.shape
    depth = w_in.shape[0]
    a_heads = sink.shape[1]
    b_heads = rpb.shape[1]
    a_w, kv_w, b_w = a_heads * HEAD_DIM, A_KV_HEADS * HEAD_DIM, b_heads * HEAD_DIM
    assert kv_w == LANES
    scale = LOG2E * HEAD_DIM ** -0.5
    win_bias = _win_bias(a_heads)
    xf = x.reshape(b * s, d)
    for l in range(depth):
        w = w_in[l]
        ka0, va0, qb0 = a_w, a_w + kv_w, a_w + 2 * kv_w
        kb0, vb0, g0 = qb0 + b_w, qb0 + 2 * b_w, qb0 + 3 * b_w
        col_scale = np.ones((1, w.shape[1]), np.float32)
        col_scale[:, :a_w] = scale
        col_scale[:, qb0:kb0] = scale
        col_of = {"qa": 0, "ka": ka0, "va": va0, "qb": qb0, "kb": kb0, "vb": vb0, "gate": g0}
        qa, ka, qb, kb, gates, vta, vtb = _in_proj(
            xf, norm_mix[l][None], (w * col_scale).astype(_BF), b_gate[l][None], col_of,
            (a_w, 2 * kv_w, b_w, b_w, 2 * d), (kv_w, b_w))
        ya = _win_attn(LOG2E * sink[l].astype(_F32), qa, ka, vta, win_bias, s)
        yb = _nbr_attn(qb, kb, vtb, _nbr_bias(LOG2E * rpb[l].astype(_F32)), s)
        xf = _out_mlp(xf, ya, yb, gates, w_proj_a[l].astype(_BF), w_proj_b[l].astype(_BF), w_out[l].astype(_BF),
                      norm_mlp[l][None], w_up[l].astype(_BF), w_down[l].astype(_BF), norm_final[None],
                      final_norm=(l == depth - 1))
    return xf.reshape(b, s, d)
```
